```python
import math
import jax, jax.numpy as jnp
from jax import lax
import numpy as np

D_MODEL = 4096
BATCH = 4
SEQ = 4096
DEPTH = 1

CHUNK = 64
MIX_WIDTH = D_MODEL
RET_WIDTH = MIX_WIDTH // 2
RET_HEADS = 8
RET_HEAD_DIM = RET_WIDTH // RET_HEADS
GDN_WIDTH = MIX_WIDTH - RET_WIDTH
GDN_HEAD_DIM = 128
GDN_HEADS = GDN_WIDTH // GDN_HEAD_DIM
CONV_WIDTH = 4
D_FF = 4 * D_MODEL
ROPE_BASE = 10000.0
EPS = 1e-6
RET_COLS = 4 * RET_WIDTH
GDN_QKV_COLS = 3 * GDN_WIDTH
GDN_COLS = GDN_QKV_COLS + GDN_WIDTH + 2 * GDN_HEADS
IN_COLS = RET_COLS + GDN_COLS

kernel_name = "hybrid_retention_gated_deltanet_block"


def _rmsnorm(x, gain):
    xf = x.astype(jnp.float32)
    y = xf * lax.rsqrt(jnp.mean(xf * xf, axis=-1, keepdims=True) + EPS)
    return (y * gain.astype(jnp.float32)).astype(x.dtype)


def _heads(t, n_heads):
    B, T, _ = t.shape
    return t.reshape(B, T, n_heads, -1).transpose(0, 2, 1, 3)


def _merge_heads(t):
    B, H, T, d = t.shape
    return t.transpose(0, 2, 1, 3).reshape(B, T, H * d)


def _rope(t):
    T, d = t.shape[2], t.shape[3]
    inv = ROPE_BASE ** (-jnp.arange(d // 2, dtype=jnp.float32) * (2.0 / d))
    ang = jnp.arange(T, dtype=jnp.float32)[:, None] * inv[None, :]
    cos = jnp.cos(ang).astype(t.dtype)
    sin = jnp.sin(ang).astype(t.dtype)
    t1, t2 = t[..., : d // 2], t[..., d // 2:]
    return jnp.concatenate([t1 * cos - t2 * sin, t1 * sin + t2 * cos], axis=-1)


def _retention(q, k, v):
    B, H, T, d = q.shape
    N = T // CHUNK
    dt = q.dtype
    log_gamma = jnp.log1p(-jnp.exp2(-5.0 - jnp.arange(H, dtype=jnp.float32)))
    pos = jnp.arange(CHUNK, dtype=jnp.float32)
    d_sym = jnp.exp(log_gamma[:, None, None] * jnp.abs(pos[:, None] - pos[None, :]))
    xi = jnp.exp(log_gamma[:, None] * (pos + 1.0))
    zeta = jnp.exp(log_gamma[:, None] * (CHUNK - 1.0 - pos))
    gamma_chunk = jnp.exp(log_gamma * CHUNK)
    q = _rope(q)
    k = _rope(k) * (d ** -0.5)
    qc, kc, vc = (t.reshape(B, H, N, CHUNK, -1) for t in (q, k, v))
    scores = jnp.einsum('bhncd,bhnmd->bhncm', qc, kc) * d_sym[:, None].astype(dt)
    o_intra = jnp.einsum('bhncm,bhnme->bhnce', scores, vc)
    k_dec = kc * zeta[:, None, :, None].astype(dt)
    xi_b = xi[None, :, :, None].astype(dt)
    gc_b = gamma_chunk[None, :, None, None].astype(dt)

    def step(R, inp):
        q_n, k_n, v_n = inp
        o = jnp.einsum('bhcd,bhde->bhce', q_n, R) * xi_b
        R = R * gc_b + jnp.einsum('bhcd,bhce->bhde', k_n, v_n)
        return R, o

    R0 = jnp.zeros((B, H, d, v.shape[-1]), dt)
    xs = (jnp.moveaxis(qc, 2, 0), jnp.moveaxis(k_dec, 2, 0), jnp.moveaxis(vc, 2, 0))
    _, o_inter = lax.scan(step, R0, xs)
    o = o_intra + jnp.moveaxis(o_inter, 0, 2)
    return o.reshape(B, H, T, -1)


def _head_groupnorm(o, gain):
    of = o.astype(jnp.float32)
    mu = jnp.mean(of, axis=-1, keepdims=True)
    var = jnp.mean(jnp.square(of - mu), axis=-1, keepdims=True)
    y = _merge_heads((of - mu) * lax.rsqrt(var + EPS))
    return (y * gain.astype(jnp.float32)).astype(o.dtype)


def _causal_conv(u, w):
    C = u.shape[-1]
    return lax.conv_general_dilated(
        u, w[:, None, :].astype(u.dtype), window_strides=(1,),
        padding=[(CONV_WIDTH - 1, 0)], dimension_numbers=('NWC', 'WIO', 'NWC'),
        feature_group_count=C)


def _gated_delta(q, k, v, g, beta):
    B, H, T, d = q.shape
    dv = v.shape[-1]
    N = T // CHUNK
    dt = q.dtype
    f32 = jnp.float32
    qc, kc, vc = (t.reshape(B, H, N, CHUNK, -1) for t in (q, k, v))
    gc = jnp.cumsum(g.reshape(B, H, N, CHUNK), axis=-1)
    bc = beta.reshape(B, H, N, CHUNK)
    idx = jnp.arange(CHUNK)
    causal = idx[:, None] >= idx[None, :]
    strict = idx[:, None] > idx[None, :]
    decay = jnp.exp(jnp.where(causal, gc[..., :, None] - gc[..., None, :], -jnp.inf))
    kf = kc.astype(f32)
    k_beta = kf * bc[..., None]
    lower = jnp.where(strict, jnp.einsum('bhncd,bhnmd->bhncm', k_beta, kf) * decay, 0.0)
    a = lower + jnp.eye(CHUNK, dtype=f32)
    rhs = jnp.concatenate([vc.astype(f32) * bc[..., None], k_beta * jnp.exp(gc)[..., None]], axis=-1)
    sol = lax.linalg.triangular_solve(a, rhs, left_side=True, lower=True, unit_diagonal=True)
    u = sol[..., :dv].astype(dt)
    w = sol[..., dv:].astype(dt)
    attn = jnp.einsum('bhncd,bhnmd->bhncm', qc, kc) * decay.astype(dt)
    q_dec = qc * jnp.exp(gc)[..., None].astype(dt)
    k_dec = kc * jnp.exp(gc[..., -1:] - gc)[..., None].astype(dt)
    chunk_decay = jnp.exp(gc[..., -1]).astype(dt)

    def step(S, inp):
        q_n, k_n, u_n, w_n, a_n, cd = inp
        v_new = u_n - jnp.einsum('bhcd,bhde->bhce', w_n, S)
        o = jnp.einsum('bhcd,bhde->bhce', q_n, S) + jnp.einsum('bhcm,bhme->bhce', a_n, v_new)
        S = S * cd[..., None, None] + jnp.einsum('bhcd,bhce->bhde', k_n, v_new)
        return S, o

    xs = tuple(jnp.moveaxis(t, 2, 0) for t in (q_dec, k_dec, u, w, attn, chunk_decay))
    S0 = jnp.zeros((B, H, d, dv), dt)
    _, o = lax.scan(step, S0, xs)
    return jnp.moveaxis(o, 0, 2).reshape(B, H, T, dv)


def _l2norm(t):
    tf = t.astype(jnp.float32)
    return (tf * lax.rsqrt(jnp.sum(tf * tf, axis=-1, keepdims=True) + EPS)).astype(t.dtype)


def setup_inputs(seed: int = 0) -> dict:
    key = jax.random.key(seed)
    ks = jax.random.split(key, 16)
    f32 = jnp.float32
    n = lambda k, shape: jax.random.normal(k, shape, f32)
    return {
        "x": n(ks[0], (BATCH, SEQ, D_MODEL)),
        "ln1_gain": 1.0 + 0.02 * n(ks[1], (DEPTH, D_MODEL)),
        "w_in": n(ks[2], (DEPTH, D_MODEL, IN_COLS)) * (D_MODEL ** -0.5),
        "ret_norm_gain": 1.0 + 0.02 * n(ks[3], (DEPTH, RET_WIDTH)),
        "gdn_conv_w": n(ks[4], (DEPTH, CONV_WIDTH, GDN_QKV_COLS)) * (CONV_WIDTH ** -0.5),
        "gdn_A_log": jnp.log(jax.random.uniform(ks[5], (DEPTH, GDN_HEADS), f32, 1.0, 16.0)),
        "gdn_dt_bias": 0.1 * n(ks[6], (DEPTH, GDN_HEADS)),
        "gdn_norm_gain": 1.0 + 0.02 * n(ks[7], (DEPTH, GDN_HEAD_DIM)),
        "w_out": n(ks[8], (DEPTH, MIX_WIDTH, D_MODEL)) * (MIX_WIDTH ** -0.5),
        "ln2_gain": 1.0 + 0.02 * n(ks[9], (DEPTH, D_MODEL)),
        "w_up": n(ks[10], (DEPTH, D_MODEL, D_FF)) * (D_MODEL ** -0.5),
        "w_down": n(ks[11], (DEPTH, D_FF, D_MODEL)) * (D_FF ** -0.5),
        "final_gain": 1.0 + 0.02 * n(ks[12], (D_MODEL,)),
    }


def reference(x, ln1_gain, w_in, ret_norm_gain, gdn_conv_w, gdn_A_log, gdn_dt_bias,
              gdn_norm_gain, w_out, ln2_gain, w_up, w_down, final_gain):
    f32 = jnp.float32
    B, T, _ = x.shape
    for l in range(DEPTH):
        h = _rmsnorm(x, ln1_gain[l])
        proj = h @ w_in[l]
        splits = np.cumsum([RET_WIDTH, RET_WIDTH, RET_WIDTH, RET_WIDTH,
                            GDN_QKV_COLS, GDN_WIDTH, GDN_HEADS]).tolist()
        rq, rk, rv, rg, gqkv, gz, gb, ga = jnp.split(proj, splits, axis=-1)

        o_ret = _retention(_heads(rq, RET_HEADS), _heads(rk, RET_HEADS), _heads(rv, RET_HEADS))
        y_ret = _head_groupnorm(o_ret, ret_norm_gain[l]) * jax.nn.silu(rg)

        cqkv = jax.nn.silu(_causal_conv(gqkv, gdn_conv_w[l]))
        cq, ck, cv = jnp.split(cqkv, [GDN_WIDTH, 2 * GDN_WIDTH], axis=-1)
        dq = _l2norm(_heads(cq, GDN_HEADS)) * (GDN_HEAD_DIM ** -0.5)
        dk = _l2norm(_heads(ck, GDN_HEADS))
        dv = _heads(cv, GDN_HEADS)
        beta = jax.nn.sigmoid(gb.astype(f32)).transpose(0, 2, 1)
        g = (-jnp.exp(gdn_A_log[l].astype(f32))
             * jax.nn.softplus(ga.astype(f32) + gdn_dt_bias[l].astype(f32))).transpose(0, 2, 1)
        o_gdn = _gated_delta(dq, dk, dv, g, beta)
        of = o_gdn.astype(f32)
        o_n = of * lax.rsqrt(jnp.mean(of * of, axis=-1, keepdims=True) + EPS) * gdn_norm_gain[l].astype(f32)
        y_gdn = _merge_heads(o_n.astype(x.dtype)) * jax.nn.silu(gz)

        mix = jnp.concatenate([y_ret, y_gdn], axis=-1) @ w_out[l]
        x = x + mix

        h2 = _rmsnorm(x, ln2_gain[l])
        x = x + jnp.square(jax.nn.relu(h2 @ w_up[l])) @ w_down[l]
    return _rmsnorm(x, final_gain)
```

```python
import functools

import numpy as np
import jax
import jax.numpy as jnp
from jax import lax
from jax.experimental import pallas as pl
from jax.experimental.pallas import tpu as pltpu

F32 = jnp.float32
BF16 = jnp.bfloat16

CHUNK = 64
RET_HEADS = 8
GDN_HEAD_DIM = 128
CONV_WIDTH = 4
ROPE_BASE = 10000.0
EPS = 1e-6

V7X_VMEM_LIMIT_BYTES = 56 * 1024 * 1024
LANES = 128
TILE = 256
NEG_BIG = -1e30

_NT = (((1,), (1,)), ((), ()))
_TN = (((0,), (0,)), ((), ()))


def _params(*sem):
    return pltpu.CompilerParams(dimension_semantics=sem, vmem_limit_bytes=V7X_VMEM_LIMIT_BYTES)


def _dot(a, b, dims=None):
    if dims is None:
        return jnp.dot(a, b, preferred_element_type=F32)
    return lax.dot_general(a, b, dims, preferred_element_type=F32)


def _silu(x):
    return x * jax.nn.sigmoid(x)


def _rmsnorm_body(x_ref, g_ref, o_ref):
    x = x_ref[...].astype(F32)
    ms = jnp.mean(x * x, axis=-1, keepdims=True)
    o_ref[...] = (x * lax.rsqrt(ms + EPS) * g_ref[...]).astype(o_ref.dtype)


def _rmsnorm(x, gain, out_dtype, tm=512):
    m, d = x.shape
    tm = min(tm, m)
    return pl.pallas_call(
        _rmsnorm_body,
        grid=(m // tm,),
        in_specs=[pl.BlockSpec((tm, d), lambda i: (i, 0)),
                  pl.BlockSpec((1, d), lambda i: (0, 0))],
        out_specs=pl.BlockSpec((tm, d), lambda i: (i, 0)),
        out_shape=jax.ShapeDtypeStruct((m, d), out_dtype),
        compiler_params=_params("parallel"),
        name="rmsnorm",
    )(x, gain.reshape(1, d).astype(F32))


def _mm_body(a_ref, b_ref, o_ref, *, relu2):
    acc = _dot(a_ref[...], b_ref[...])
    if relu2:
        acc = jnp.square(jnp.maximum(acc, 0.0))
    o_ref[...] = acc.astype(o_ref.dtype)


def _matmul(a, b, out_dtype, *, relu2=False, tm=1024, tn=1024, name="matmul"):
    m, k = a.shape
    n = b.shape[1]
    tm, tn = min(tm, m), min(tn, n)
    return pl.pallas_call(
        functools.partial(_mm_body, relu2=relu2),
        grid=(m // tm, n // tn),
        in_specs=[pl.BlockSpec((tm, k), lambda i, j: (i, 0)),
                  pl.BlockSpec((k, tn), lambda i, j: (0, j))],
        out_specs=pl.BlockSpec((tm, tn), lambda i, j: (i, j)),
        out_shape=jax.ShapeDtypeStruct((m, n), out_dtype),
        compiler_params=_params("parallel", "arbitrary"),
        name=name,
    )(a, b)


def _out_proj_body(a1_ref, a2_ref, b1_ref, b2_ref, x_ref, o_ref):
    acc = _dot(a1_ref[...], b1_ref[...]) + _dot(a2_ref[...], b2_ref[...])
    o_ref[...] = x_ref[...] + acc


def _out_proj(y1, y2, w1, w2, x, tm=512, tn=1024):
    m, k1 = y1.shape
    k2 = y2.shape[1]
    n = w1.shape[1]
    tm, tn = min(tm, m), min(tn, n)
    return pl.pallas_call(
        _out_proj_body,
        grid=(m // tm, n // tn),
        in_specs=[pl.BlockSpec((tm, k1), lambda i, j: (i, 0)),
                  pl.BlockSpec((tm, k2), lambda i, j: (i, 0)),
                  pl.BlockSpec((k1, tn), lambda i, j: (0, j)),
                  pl.BlockSpec((k2, tn), lambda i, j: (0, j)),
                  pl.BlockSpec((tm, tn), lambda i, j: (i, j))],
        out_specs=pl.BlockSpec((tm, tn), lambda i, j: (i, j)),
        out_shape=jax.ShapeDtypeStruct((m, n), F32),
        compiler_params=_params("parallel", "arbitrary"),
        name="out_proj",
    )(y1, y2, w1, w2, x)


def _down_body(a_ref, b_ref, x_ref, o_ref, acc_ref):
    kk = pl.program_id(2)
    part = _dot(a_ref[...], b_ref[...])

    @pl.when(kk == 0)
    def _():
        acc_ref[...] = part

    @pl.when(kk > 0)
    def _():
        acc_ref[...] += part

    @pl.when(kk == pl.num_programs(2) - 1)
    def _():
        o_ref[...] = x_ref[...] + acc_ref[...]


def _down_proj(a, b, x, tm=1024, tn=1024, tk=2048):
    m, k = a.shape
    n = b.shape[1]
    tm, tn, tk = min(tm, m), min(tn, n), min(tk, k)
    return pl.pallas_call(
        _down_body,
        grid=(m // tm, n // tn, k // tk),
        in_specs=[pl.BlockSpec((tm, tk), lambda i, j, kk: (i, kk)),
                  pl.BlockSpec((tk, tn), lambda i, j, kk: (kk, j)),
                  pl.BlockSpec((tm, tn), lambda i, j, kk: (i, j))],
        out_specs=pl.BlockSpec((tm, tn), lambda i, j, kk: (i, j)),
        out_shape=jax.ShapeDtypeStruct((m, n), F32),
        scratch_shapes=[pltpu.VMEM((tm, tn), F32)],
        compiler_params=_params("parallel", "parallel", "arbitrary"),
        name="down_proj",
    )(a, b, x)


def _split3(x):
    hi = x.astype(BF16)
    r1 = x - hi.astype(F32)
    mid = r1.astype(BF16)
    lo = (r1 - mid.astype(F32)).astype(BF16)
    return hi, mid, lo


def _gates_body(h_ref, wt_ref, alog_ref, dtb_ref, ucum_ref, ublk_ref, o_ref, *, nh):
    logits = _dot(wt_ref[...], h_ref[...], _NT)
    beta = jax.nn.sigmoid(logits[:nh])
    g = -jnp.exp(alog_ref[...]) * jax.nn.softplus(logits[nh:] + dtb_ref[...])
    parts = _split3(g)
    ucum, ublk = ucum_ref[...], ublk_ref[...]
    gc = (_dot(parts[0], ucum) + _dot(parts[1], ucum)) + _dot(parts[2], ucum)
    gl = (_dot(parts[0], ublk) + _dot(parts[1], ublk)) + _dot(parts[2], ublk)
    o_ref[0] = beta
    o_ref[1] = gc
    o_ref[2] = jnp.exp(gc)
    o_ref[3] = jnp.exp(gl - gc)
    o_ref[4] = jnp.exp(gl)
    zero = jnp.zeros_like(beta)
    o_ref[5] = zero
    o_ref[6] = zero
    o_ref[7] = zero


def _gdn_gates(h3, w_tail_t, a_log, dt_bias, tt=512):
    bsz, t, d = h3.shape
    nh = a_log.shape[0]
    tt = min(tt, t)
    pos = np.arange(tt)
    same = (pos[:, None] // CHUNK) == (pos[None, :] // CHUNK)
    ucum = jnp.asarray(same & (pos[:, None] <= pos[None, :]), BF16)
    ublk = jnp.asarray(same, BF16)
    return pl.pallas_call(
        functools.partial(_gates_body, nh=nh),
        grid=(bsz, t // tt),
        in_specs=[pl.BlockSpec((None, tt, d), lambda b, i: (b, i, 0)),
                  pl.BlockSpec((2 * nh, d), lambda b, i: (0, 0)),
                  pl.BlockSpec((nh, 1), lambda b, i: (0, 0)),
                  pl.BlockSpec((nh, 1), lambda b, i: (0, 0)),
                  pl.BlockSpec((tt, tt), lambda b, i: (0, 0)),
                  pl.BlockSpec((tt, tt), lambda b, i: (0, 0))],
        out_specs=pl.BlockSpec((None, 8, nh, tt), lambda b, i: (b, 0, 0, i)),
        out_shape=jax.ShapeDtypeStruct((bsz, 8, nh, t), F32),
        compiler_params=_params("parallel", "parallel"),
        name="gdn_gates",
    )(h3, w_tail_t, a_log.reshape(nh, 1).astype(F32), dt_bias.reshape(nh, 1).astype(F32), ucum, ublk)


def _retention_tables(t, hd):
    half = hd // 2
    inv = ROPE_BASE ** (-np.arange(half, dtype=np.float64) * (2.0 / hd))
    ang = np.arange(t, dtype=np.float64)[:, None] * inv[None, :]
    lg = np.log1p(-np.exp2(-5.0 - np.arange(RET_HEADS, dtype=np.float64)))[:, None, None]
    i = np.arange(TILE)[:, None]
    j = np.arange(TILE)[None, :]
    dmask = np.exp(lg * np.abs(i - j)) * ((j // CHUNK) <= (i // CHUNK))
    xi = np.broadcast_to(np.exp(lg * (i + 1.0)), (RET_HEADS, TILE, hd))
    zeta = np.broadcast_to(np.exp(lg * (TILE - 1.0 - i)), (RET_HEADS, TILE, hd))
    gtile = np.broadcast_to(np.exp(lg * TILE), (RET_HEADS, 1, hd))
    f = lambda a: jnp.asarray(np.ascontiguousarray(a), F32)
    return f(np.cos(ang)), f(np.sin(ang)), f(dmask), f(xi), f(zeta), f(gtile)


def _ret_body(q_ref, k_ref, v_ref, g_ref, cos_ref, sin_ref, dm_ref, xi_ref, zeta_ref, gt_ref,
              gain_ref, o_ref, state_ref, *, hd):
    @pl.when(pl.program_id(2) == 0)
    def _():
        state_ref[...] = jnp.zeros_like(state_ref)

    half = hd // 2
    cos, sin = cos_ref[...], sin_ref[...]

    def rope(x):
        x1, x2 = x[:, :half], x[:, half:]
        return jnp.concatenate([x1 * cos - x2 * sin, x1 * sin + x2 * cos], axis=-1)

    q = rope(q_ref[...].astype(F32))
    k = rope(k_ref[...].astype(F32)) * (hd ** -0.5)
    v = v_ref[...].astype(BF16)
    qb = q.astype(BF16)
    scores = _dot(qb, k.astype(BF16), _NT) * dm_ref[...]
    state = state_ref[...]
    o = _dot(scores.astype(BF16), v) + xi_ref[...] * _dot(qb, state.astype(BF16))
    k_dec = (k * zeta_ref[...]).astype(BF16)
    state_ref[...] = state * gt_ref[...] + _dot(k_dec, v, _TN)

    mu = jnp.mean(o, axis=-1, keepdims=True)
    oc = o - mu
    var = jnp.mean(oc * oc, axis=-1, keepdims=True)
    y = oc * lax.rsqrt(var + EPS) * gain_ref[...]
    o_ref[...] = (y * _silu(g_ref[...].astype(F32))).astype(o_ref.dtype)


def _retention(proj3, ret_gain, ret_width):
    bsz, t, _ = proj3.shape
    hd = ret_width // RET_HEADS
    nb = ret_width // hd
    cos, sin, dmask, xi, zeta, gtile = _retention_tables(t, hd)
    col = lambda off: pl.BlockSpec((None, TILE, hd), lambda b, h, i: (b, i, off + h))
    head = lambda r: pl.BlockSpec((None, r, hd), lambda b, h, i: (h, 0, 0))
    return pl.pallas_call(
        functools.partial(_ret_body, hd=hd),
        grid=(bsz, RET_HEADS, t // TILE),
        in_specs=[col(0), col(nb), col(2 * nb), col(3 * nb),
                  pl.BlockSpec((TILE, hd // 2), lambda b, h, i: (i, 0)),
                  pl.BlockSpec((TILE, hd // 2), lambda b, h, i: (i, 0)),
                  pl.BlockSpec((None, TILE, TILE), lambda b, h, i: (h, 0, 0)),
                  head(TILE), head(TILE), head(1),
                  pl.BlockSpec((1, hd), lambda b, h, i: (0, h))],
        out_specs=pl.BlockSpec((None, TILE, hd), lambda b, h, i: (b, i, h)),
        out_shape=jax.ShapeDtypeStruct((bsz, t, ret_width), BF16),
        scratch_shapes=[pltpu.VMEM((hd, hd), F32)],
        compiler_params=_params("parallel", "parallel", "arbitrary"),
        name="retention",
    )(proj3, proj3, proj3, proj3, cos, sin, dmask, xi, zeta, gtile,
      ret_gain.reshape(1, ret_width).astype(F32))


def _prep_body(x_ref, w_ref, o_ref, xbuf_ref, *, tt, n_qk_blocks, n_q_blocks):
    halo = 8

    @pl.when(pl.program_id(2) == 0)
    def _():
        xbuf_ref[0:halo, :] = jnp.zeros((halo, xbuf_ref.shape[1]), F32)

    x = x_ref[...].astype(F32)
    xbuf_ref[halo:halo + tt, :] = x
    w = w_ref[...]
    acc = w[CONV_WIDTH - 1:CONV_WIDTH, :] * x
    for j in range(CONV_WIDTH - 1):
        acc = acc + w[j:j + 1, :] * xbuf_ref[pl.ds(halo - (CONV_WIDTH - 1) + j, tt), :]
    xbuf_ref[0:halo, :] = x[tt - halo:tt, :]
    y = _silu(acc)
    c = pl.program_id(1)

    @pl.when(c < n_qk_blocks)
    def _():
        scale = jnp.where(c < n_q_blocks, GDN_HEAD_DIM ** -0.5, 1.0).astype(F32)
        for s in range(y.shape[1] // GDN_HEAD_DIM):
            sl = slice(s * GDN_HEAD_DIM, (s + 1) * GDN_HEAD_DIM)
            yh = y[:, sl]
            ss = jnp.sum(yh * yh, axis=-1, keepdims=True)
            o_ref[:, sl] = (yh * lax.rsqrt(ss + EPS) * scale).astype(o_ref.dtype)

    @pl.when(c >= n_qk_blocks)
    def _():
        o_ref[...] = y.astype(o_ref.dtype)


def _gdn_prep(proj3, conv_w, col_off, width, tt=512, cb=512):
    bsz, t, _ = proj3.shape
    tt = min(tt, t)
    ncb = 3 * width // cb
    return pl.pallas_call(
        functools.partial(_prep_body, tt=tt, n_qk_blocks=2 * width // cb, n_q_blocks=width // cb),
        grid=(bsz, ncb, t // tt),
        in_specs=[pl.BlockSpec((None, tt, cb), lambda b, c, i: (b, i, col_off // cb + c)),
                  pl.BlockSpec((CONV_WIDTH, cb), lambda b, c, i: (0, c))],
        out_specs=pl.BlockSpec((None, tt, cb), lambda b, c, i: (b, i, c)),
        out_shape=jax.ShapeDtypeStruct((bsz, t, 3 * width), BF16),
        scratch_shapes=[pltpu.VMEM((tt + 8, cb), F32)],
        compiler_params=_params("parallel", "parallel", "arbitrary"),
        name="gdn_prep",
    )(proj3, conv_w.astype(F32))


def _gdn_body(q_ref, k_ref, v_ref, z_ref, gate_ref, incl_ref, strict_ref, gain_ref, o_ref, state_ref,
              *, hb):
    d = GDN_HEAD_DIM
    nck = TILE // CHUNK

    @pl.when(pl.program_id(2) == 0)
    def _():
        state_ref[...] = jnp.zeros_like(state_ref)

    rows = jnp.concatenate([gate_ref[c] for c in range(5)]
                           + [jnp.zeros((LANES - 5 * hb, TILE), F32)], axis=0)
    cols = rows.T
    incl = incl_ref[...] > 0.0
    strict = strict_ref[...]
    eye = (lax.broadcasted_iota(jnp.int32, (TILE, TILE), 0)
           == lax.broadcasted_iota(jnp.int32, (TILE, TILE), 1)).astype(F32)
    gain = gain_ref[...]

    for l in range(hb):
        col = lambda c: cols[:, c * hb + l:c * hb + l + 1]
        beta, gc_c, eg, kd = col(0), col(1), col(2), col(3)
        gc_r = gate_ref[1, l:l + 1, :]
        sl = slice(l * d, (l + 1) * d)
        q = q_ref[:, sl].astype(F32)
        kb = k_ref[:, sl]
        k = kb.astype(F32)
        v = v_ref[:, sl].astype(F32)

        decay = jnp.exp(jnp.where(incl, gc_c - gc_r, NEG_BIG))
        k_beta = k * beta
        kk = _dot(k_beta.astype(BF16), kb, _NT)
        p = -(kk * decay * strict)
        inv = eye + p
        pw = p.astype(BF16)
        for _ in range(5):
            pw32 = _dot(pw, pw)
            pw = pw32.astype(BF16)
            inv = inv + _dot(inv.astype(BF16), pw)
        rhs = jnp.concatenate([v * beta, k_beta * eg], axis=-1).astype(BF16)
        sol = _dot(inv.astype(BF16), rhs).astype(BF16)
        attn = (_dot(q.astype(BF16), kb, _NT) * decay).astype(BF16)
        au_aw = _dot(attn, sol)
        au, aw = au_aw[:, :d], au_aw[:, d:]
        q_eff = (q * eg - aw).astype(BF16)
        k_dec = (k * kd).astype(BF16)

        state = state_ref[l]
        outs = []
        for c in range(nck):
            r = slice(c * CHUNK, (c + 1) * CHUNK)
            ktuw = _dot(k_dec[r], sol[r], _TN)
            sb = state.astype(BF16)
            outs.append(_dot(q_eff[r], sb) + au[r])
            cd = cols[c * CHUNK:c * CHUNK + 1, 4 * hb + l:4 * hb + l + 1]
            state = state * cd + ktuw[:, :d] - _dot(ktuw[:, d:].astype(BF16), sb)
        state_ref[l] = state
        o = jnp.concatenate(outs, axis=0)
        o = o * lax.rsqrt(jnp.mean(o * o, axis=-1, keepdims=True) + EPS) * gain
        o_ref[:, sl] = (o * _silu(z_ref[:, sl].astype(F32))).astype(o_ref.dtype)


def _gated_delta(qkv3, proj3, z_col_off, gates, norm_gain, width, hb=8):
    bsz, t, _ = qkv3.shape
    d = GDN_HEAD_DIM
    nh = width // d
    hb = min(hb, nh)
    nhb = nh // hb
    cw = hb * d
    i = np.arange(TILE)[:, None]
    j = np.arange(TILE)[None, :]
    same = (i // CHUNK) == (j // CHUNK)
    incl = jnp.asarray(same & (i >= j), F32)
    strict = jnp.asarray(same & (i > j), F32)
    col = lambda a, off: pl.BlockSpec((None, TILE, cw), lambda b, h, s: (b, s, off // cw + h))
    const = pl.BlockSpec((TILE, TILE), lambda b, h, s: (0, 0))
    return pl.pallas_call(
        functools.partial(_gdn_body, hb=hb),
        grid=(bsz, nhb, t // TILE),
        in_specs=[col(qkv3, 0), col(qkv3, width), col(qkv3, 2 * width), col(proj3, z_col_off),
                  pl.BlockSpec((None, 8, hb, TILE), lambda b, h, s: (b, 0, h, s)),
                  const, const,
                  pl.BlockSpec((1, d), lambda b, h, s: (0, 0))],
        out_specs=pl.BlockSpec((None, TILE, cw), lambda b, h, s: (b, s, h)),
        out_shape=jax.ShapeDtypeStruct((bsz, t, width), BF16),
        scratch_shapes=[pltpu.VMEM((hb, d, d), F32)],
        compiler_params=_params("parallel", "parallel", "arbitrary"),
        name="gated_delta",
    )(qkv3, qkv3, qkv3, proj3, gates, incl, strict, norm_gain.reshape(1, d).astype(F32))


def kernel(x, ln1_gain, w_in, ret_norm_gain, gdn_conv_w, gdn_A_log, gdn_dt_bias, gdn_norm_gain,
           w_out, ln2_gain, w_up, w_down, final_gain):
    bsz, t, dm = x.shape
    depth = w_in.shape[0]
    nh = gdn_A_log.shape[1]
    gw = nh * GDN_HEAD_DIM
    rw = w_out.shape[1] - gw
    main_cols = 4 * rw + 4 * gw
    m = bsz * t
    xf = x.reshape(m, dm)
    for l in range(depth):
        h = _rmsnorm(xf, ln1_gain[l], BF16)
        proj = _matmul(h, w_in[l, :, :main_cols].astype(BF16), F32, name="in_proj")
        proj3 = proj.reshape(bsz, t, main_cols)
        gates = _gdn_gates(h.reshape(bsz, t, dm), w_in[l, :, main_cols:].T.astype(BF16),
                           gdn_A_log[l], gdn_dt_bias[l])
        y_ret = _retention(proj3, ret_norm_gain[l], rw)
        qkv = _gdn_prep(proj3, gdn_conv_w[l], 4 * rw, gw)
        y_gdn = _gated_delta(qkv, proj3, 4 * rw + 3 * gw, gates, gdn_norm_gain[l], gw)
        w_o = w_out[l].astype(BF16)
        xf = _out_proj(y_ret.reshape(m, rw), y_gdn.reshape(m, gw), w_o[:rw], w_o[rw:], xf)
        h2 = _rmsnorm(xf, ln2_gain[l], BF16)
        act = _matmul(h2, w_up[l].astype(BF16), BF16, relu2=True, name="up_proj")
        xf = _down_proj(act, w_down[l].astype(BF16), xf)
    return _rmsnorm(xf, final_gain, F32).reshape(bsz, t, dm)
```

```python
import functools

import numpy as np
import jax
import jax.numpy as jnp
from jax import lax
from jax.experimental import pallas as pl
from jax.experimental.pallas import tpu as pltpu

F32 = jnp.float32
BF16 = jnp.bfloat16

CHUNK = 64
RET_HEADS = 8
GDN_HEAD_DIM = 128
CONV_WIDTH = 4
ROPE_BASE = 10000.0
EPS = 1e-6

V7X_VMEM_LIMIT_BYTES = 56 * 1024 * 1024
LANES = 128
TILE = 256
NEG_BIG = -1e30

_NT = (((1,), (1,)), ((), ()))
_TN = (((0,), (0,)), ((), ()))


def _params(*sem):
    return pltpu.CompilerParams(dimension_semantics=sem, vmem_limit_bytes=V7X_VMEM_LIMIT_BYTES)


def _dot(a, b, dims=None):
    if dims is None:
        return jnp.dot(a, b, preferred_element_type=F32)
    return lax.dot_general(a, b, dims, preferred_element_type=F32)


def _silu(x):
    return x * jax.nn.sigmoid(x)


def _rmsnorm_body(x_ref, g_ref, o_ref):
    x = x_ref[...].astype(F32)
    ms = jnp.mean(x * x, axis=-1, keepdims=True)
    o_ref[...] = (x * lax.rsqrt(ms + EPS) * g_ref[...]).astype(o_ref.dtype)


def _rmsnorm(x, gain, out_dtype, tm=512):
    m, d = x.shape
    tm = min(tm, m)
    return pl.pallas_call(
        _rmsnorm_body,
        grid=(m // tm,),
        in_specs=[pl.BlockSpec((tm, d), lambda i: (i, 0)),
                  pl.BlockSpec((1, d), lambda i: (0, 0))],
        out_specs=pl.BlockSpec((tm, d), lambda i: (i, 0)),
        out_shape=jax.ShapeDtypeStruct((m, d), out_dtype),
        compiler_params=_params("parallel"),
        name="rmsnorm",
    )(x, gain.reshape(1, d).astype(F32))


def _mm_body(a_ref, b_ref, o_ref, *, relu2):
    acc = _dot(a_ref[...], b_ref[...])
    if relu2:
        acc = jnp.square(jnp.maximum(acc, 0.0))
    o_ref[...] = acc.astype(o_ref.dtype)


def _matmul(a, b, out_dtype, *, n=None, relu2=False, tm=1024, tn=1024, name="matmul"):
    m, k = a.shape
    n = b.shape[1] if n is None else n
    tm, tn = min(tm, m), min(tn, n)
    assert m % tm == 0 and n % tn == 0
    return pl.pallas_call(
        functools.partial(_mm_body, relu2=relu2),
        grid=(m // tm, n // tn),
        in_specs=[pl.BlockSpec((tm, k), lambda i, j: (i, 0)),
                  pl.BlockSpec((k, tn), lambda i, j: (0, j))],
        out_specs=pl.BlockSpec((tm, tn), lambda i, j: (i, j)),
        out_shape=jax.ShapeDtypeStruct((m, n), out_dtype),
        compiler_params=_params("parallel", "arbitrary"),
        name=name,
    )(a, b)


def _out_proj_body(a1_ref, a2_ref, b1_ref, b2_ref, x_ref, o_ref):
    acc = _dot(a1_ref[...], b1_ref[...]) + _dot(a2_ref[...], b2_ref[...])
    o_ref[...] = x_ref[...] + acc


def _out_proj(y1, y2, w, x, tm=1024, tn=1024):
    m, k1 = y1.shape
    assert y2.shape[1] == k1 and w.shape[0] == 2 * k1
    n = w.shape[1]
    tm, tn = min(tm, m), min(tn, n)
    return pl.pallas_call(
        _out_proj_body,
        grid=(m // tm, n // tn),
        in_specs=[pl.BlockSpec((tm, k1), lambda i, j: (i, 0)),
                  pl.BlockSpec((tm, k1), lambda i, j: (i, 0)),
                  pl.BlockSpec((k1, tn), lambda i, j: (0, j)),
                  pl.BlockSpec((k1, tn), lambda i, j: (1, j)),
                  pl.BlockSpec((tm, tn), lambda i, j: (i, j))],
        out_specs=pl.BlockSpec((tm, tn), lambda i, j: (i, j)),
        out_shape=jax.ShapeDtypeStruct((m, n), F32),
        compiler_params=_params("parallel", "arbitrary"),
        name="out_proj",
    )(y1, y2, w, w, x)


def _down_body(a_ref, b_ref, x_ref, o_ref):
    @pl.when(pl.program_id(2) == 0)
    def _():
        o_ref[...] = x_ref[...]

    o_ref[...] += _dot(a_ref[...], b_ref[...])


def _down_proj(a, b, x, tm=1024, tn=1024, tk=2048):
    m, k = a.shape
    n = b.shape[1]
    tm, tn, tk = min(tm, m), min(tn, n), min(tk, k)
    return pl.pallas_call(
        _down_body,
        grid=(m // tm, n // tn, k // tk),
        in_specs=[pl.BlockSpec((tm, tk), lambda i, j, kk: (i, kk)),
                  pl.BlockSpec((tk, tn), lambda i, j, kk: (kk, j)),
                  pl.BlockSpec((tm, tn), lambda i, j, kk: (i, j))],
        out_specs=pl.BlockSpec((tm, tn), lambda i, j, kk: (i, j)),
        out_shape=jax.ShapeDtypeStruct((m, n), F32),
        compiler_params=_params("parallel", "parallel", "arbitrary"),
        name="down_proj",
    )(a, b, x)


def _split3(x):
    hi = x.astype(BF16)
    r1 = x - hi.astype(F32)
    mid = r1.astype(BF16)
    lo = (r1 - mid.astype(F32)).astype(BF16)
    return hi, mid, lo


def _gates_body(h_ref, wt_ref, alog_ref, dtb_ref, ucum_ref, ublk_ref, o_ref, *, nh):
    logits = _dot(wt_ref[...], h_ref[...], _NT)
    beta = jax.nn.sigmoid(logits[:nh])
    g = -jnp.exp(alog_ref[...]) * jax.nn.softplus(logits[nh:] + dtb_ref[...])
    parts = _split3(g)
    ucum, ublk = ucum_ref[...], ublk_ref[...]
    gc = (_dot(parts[0], ucum) + _dot(parts[1], ucum)) + _dot(parts[2], ucum)
    gl = (_dot(parts[0], ublk) + _dot(parts[1], ublk)) + _dot(parts[2], ublk)
    o_ref[0] = beta
    o_ref[1] = gc
    o_ref[2] = jnp.exp(gc)
    o_ref[3] = jnp.exp(gl - gc)
    o_ref[4] = jnp.exp(gl)
    zero = jnp.zeros_like(beta)
    o_ref[5] = zero
    o_ref[6] = zero
    o_ref[7] = zero


def _gdn_gates(h3, w_tail_t, a_log, dt_bias, tt=512):
    bsz, t, d = h3.shape
    nh = a_log.shape[0]
    tt = min(tt, t)
    pos = np.arange(tt)
    same = (pos[:, None] // CHUNK) == (pos[None, :] // CHUNK)
    ucum = jnp.asarray(same & (pos[:, None] <= pos[None, :]), BF16)
    ublk = jnp.asarray(same, BF16)
    return pl.pallas_call(
        functools.partial(_gates_body, nh=nh),
        grid=(bsz, t // tt),
        in_specs=[pl.BlockSpec((None, tt, d), lambda b, i: (b, i, 0)),
                  pl.BlockSpec((2 * nh, d), lambda b, i: (0, 0)),
                  pl.BlockSpec((nh, 1), lambda b, i: (0, 0)),
                  pl.BlockSpec((nh, 1), lambda b, i: (0, 0)),
                  pl.BlockSpec((tt, tt), lambda b, i: (0, 0)),
                  pl.BlockSpec((tt, tt), lambda b, i: (0, 0))],
        out_specs=pl.BlockSpec((None, 8, nh, tt), lambda b, i: (b, 0, 0, i)),
        out_shape=jax.ShapeDtypeStruct((bsz, 8, nh, t), F32),
        compiler_params=_params("parallel", "parallel"),
        name="gdn_gates",
    )(h3, w_tail_t, a_log.reshape(nh, 1).astype(F32), dt_bias.reshape(nh, 1).astype(F32), ucum, ublk)


def _retention_tables(t, hd):
    half = hd // 2
    inv = ROPE_BASE ** (-np.arange(half, dtype=np.float64) * (2.0 / hd))
    ang = np.arange(t, dtype=np.float64)[:, None] * inv[None, :]
    lg = np.log1p(-np.exp2(-5.0 - np.arange(RET_HEADS, dtype=np.float64)))[:, None, None]
    i = np.arange(TILE)[:, None]
    j = np.arange(TILE)[None, :]
    dmask = np.exp(lg * np.abs(i - j)) * ((j // CHUNK) <= (i // CHUNK))
    xi = np.broadcast_to(np.exp(lg * (i + 1.0)), (RET_HEADS, TILE, hd))
    zeta = np.broadcast_to(np.exp(lg * (TILE - 1.0 - i)), (RET_HEADS, TILE, hd))
    gtile = np.broadcast_to(np.exp(lg * TILE), (RET_HEADS, 1, hd))
    f = lambda a: jnp.asarray(np.ascontiguousarray(a), F32)
    return f(np.cos(ang)), f(np.sin(ang)), f(dmask), f(xi), f(zeta), f(gtile)


def _ret_body(q_ref, k_ref, v_ref, g_ref, cos_ref, sin_ref, dm_ref, xi_ref, zeta_ref, gt_ref,
              gain_ref, o_ref, state_ref, *, hd, hb):
    @pl.when(pl.program_id(2) == 0)
    def _():
        state_ref[...] = jnp.zeros_like(state_ref)

    half = hd // 2
    cos, sin = cos_ref[...], sin_ref[...]

    def rope(x):
        x1, x2 = x[:, :half], x[:, half:]
        return jnp.concatenate([x1 * cos - x2 * sin, x1 * sin + x2 * cos], axis=-1)

    heads = range(hb)
    sls = [slice(l * hd, (l + 1) * hd) for l in heads]
    qb = [rope(q_ref[:, sls[l]].astype(F32)).astype(BF16) for l in heads]
    k = [rope(k_ref[:, sls[l]].astype(F32)) * (hd ** -0.5) for l in heads]
    v = [v_ref[:, sls[l]].astype(BF16) for l in heads]
    scores = [(_dot(qb[l], k[l].astype(BF16), _NT) * dm_ref[l]).astype(BF16) for l in heads]
    state = [state_ref[l] for l in heads]
    o = [_dot(scores[l], v[l]) + xi_ref[l] * _dot(qb[l], state[l].astype(BF16)) for l in heads]
    for l in heads:
        k_dec = (k[l] * zeta_ref[l]).astype(BF16)
        state_ref[l] = state[l] * gt_ref[l] + _dot(k_dec, v[l], _TN)
    for l in heads:
        mu = jnp.mean(o[l], axis=-1, keepdims=True)
        oc = o[l] - mu
        var = jnp.mean(oc * oc, axis=-1, keepdims=True)
        y = oc * lax.rsqrt(var + EPS) * gain_ref[:, sls[l]]
        o_ref[:, sls[l]] = (y * _silu(g_ref[:, sls[l]].astype(F32))).astype(o_ref.dtype)


def _retention(proj3, ret_gain, ret_width, hb=8):
    bsz, t, _ = proj3.shape
    hd = ret_width // RET_HEADS
    nhb = RET_HEADS // hb
    cw = hb * hd
    cos, sin, dmask, xi, zeta, gtile = _retention_tables(t, hd)
    col = lambda sec: pl.BlockSpec((None, TILE, cw), lambda b, h, i: (b, i, sec * nhb + h))
    head = lambda r, c: pl.BlockSpec((hb, r, c), lambda b, h, i: (h, 0, 0))
    return pl.pallas_call(
        functools.partial(_ret_body, hd=hd, hb=hb),
        grid=(bsz, nhb, t // TILE),
        in_specs=[col(0), col(1), col(2), col(3),
                  pl.BlockSpec((TILE, hd // 2), lambda b, h, i: (i, 0)),
                  pl.BlockSpec((TILE, hd // 2), lambda b, h, i: (i, 0)),
                  head(TILE, TILE), head(TILE, hd), head(TILE, hd), head(1, hd),
                  pl.BlockSpec((1, cw), lambda b, h, i: (0, h))],
        out_specs=pl.BlockSpec((None, TILE, cw), lambda b, h, i: (b, i, h)),
        out_shape=jax.ShapeDtypeStruct((bsz, t, ret_width), BF16),
        scratch_shapes=[pltpu.VMEM((hb, hd, hd), F32)],
        compiler_params=_params("parallel", "parallel", "arbitrary"),
        name="retention",
    )(proj3, proj3, proj3, proj3, cos, sin, dmask, xi, zeta, gtile,
      ret_gain.reshape(1, ret_width).astype(F32))


def _prep_body(x_ref, w_ref, o_ref, xbuf_ref, *, tt, n_qk_blocks, n_q_blocks):
    halo = 8

    @pl.when(pl.program_id(2) == 0)
    def _():
        xbuf_ref[0:halo, :] = jnp.zeros((halo, xbuf_ref.shape[1]), F32)

    x = x_ref[...].astype(F32)
    xbuf_ref[halo:halo + tt, :] = x
    w = w_ref[...]
    acc = w[CONV_WIDTH - 1:CONV_WIDTH, :] * x
    for j in range(CONV_WIDTH - 1):
        acc = acc + w[j:j + 1, :] * xbuf_ref[pl.ds(halo - (CONV_WIDTH - 1) + j, tt), :]
    xbuf_ref[0:halo, :] = x[tt - halo:tt, :]
    y = _silu(acc)
    c = pl.program_id(1)

    @pl.when(c < n_qk_blocks)
    def _():
        scale = jnp.where(c < n_q_blocks, GDN_HEAD_DIM ** -0.5, 1.0).astype(F32)
        for s in range(y.shape[1] // GDN_HEAD_DIM):
            sl = slice(s * GDN_HEAD_DIM, (s + 1) * GDN_HEAD_DIM)
            yh = y[:, sl]
            ss = jnp.sum(yh * yh, axis=-1, keepdims=True)
            o_ref[:, sl] = (yh * lax.rsqrt(ss + EPS) * scale).astype(o_ref.dtype)

    @pl.when(c >= n_qk_blocks)
    def _():
        o_ref[...] = y.astype(o_ref.dtype)


def _gdn_prep(proj3, conv_w, col_off, width, tt=512, cb=512):
    bsz, t, _ = proj3.shape
    tt = min(tt, t)
    ncb = 3 * width // cb
    return pl.pallas_call(
        functools.partial(_prep_body, tt=tt, n_qk_blocks=2 * width // cb, n_q_blocks=width // cb),
        grid=(bsz, ncb, t // tt),
        in_specs=[pl.BlockSpec((None, tt, cb), lambda b, c, i: (b, i, col_off // cb + c)),
                  pl.BlockSpec((CONV_WIDTH, cb), lambda b, c, i: (0, c))],
        out_specs=pl.BlockSpec((None, tt, cb), lambda b, c, i: (b, i, c)),
        out_shape=jax.ShapeDtypeStruct((bsz, t, 3 * width), BF16),
        scratch_shapes=[pltpu.VMEM((tt + 8, cb), F32)],
        compiler_params=_params("parallel", "parallel", "arbitrary"),
        name="gdn_prep",
    )(proj3, conv_w.astype(F32))


def _gdn_body(q_ref, k_ref, v_ref, z_ref, gate_ref, incl_ref, nstrict_ref, eye_ref, blk_ref, gain_ref,
              o_ref, state_ref, *, hb):
    d = GDN_HEAD_DIM
    nck = TILE // CHUNK

    @pl.when(pl.program_id(2) == 0)
    def _():
        state_ref[...] = jnp.zeros_like(state_ref)

    rows = jnp.concatenate([gate_ref[c] for c in range(5)]
                           + [jnp.zeros((LANES - 5 * hb, TILE), F32)], axis=0)
    cols = rows.T
    incl = incl_ref[...] > 0.0
    nstrict = nstrict_ref[...]
    eye = eye_ref[...]
    blk = blk_ref[...]
    gain = gain_ref[...]

    def block_diag(packed):
        return jnp.concatenate([packed] * nck, axis=0) * blk

    heads = range(hb)
    sls = [slice(l * d, (l + 1) * d) for l in heads]
    col = lambda c, l: cols[:, c * hb + l:c * hb + l + 1]
    kbs = [k_ref[:, sls[l]] for l in heads]
    ks = [kb.astype(F32) for kb in kbs]
    decay = [jnp.exp(jnp.where(incl, col(1, l) - gate_ref[1, l:l + 1, :], NEG_BIG)) for l in heads]
    k_beta = [ks[l] * col(0, l) for l in heads]
    kkd = [_dot(k_beta[l].astype(BF16), kbs[l], _NT) * decay[l] for l in heads]
    p = [sum(kkd[l][c * CHUNK:(c + 1) * CHUNK] for c in range(nck)) * nstrict for l in heads]
    acc = [eye + p[l] for l in heads]
    pw = [p[l].astype(BF16) for l in heads]
    pw = [_dot(pw[l], block_diag(pw[l])).astype(BF16) for l in heads]
    for _ in range(4):
        res = [_dot(jnp.concatenate([acc[l].astype(BF16), pw[l]], axis=0), block_diag(pw[l])) for l in heads]
        acc = [acc[l] + res[l][:CHUNK] for l in heads]
        pw = [res[l][CHUNK:].astype(BF16) for l in heads]
    acc = [acc[l] + _dot(acc[l].astype(BF16), block_diag(pw[l])) for l in heads]
    rhs = [jnp.concatenate([v_ref[:, sls[l]].astype(F32) * col(0, l), k_beta[l] * col(2, l)],
                           axis=-1).astype(BF16) for l in heads]
    sol = [_dot(block_diag(acc[l].astype(BF16)), rhs[l]).astype(BF16) for l in heads]
    qs = [q_ref[:, sls[l]].astype(F32) for l in heads]
    attn = [(_dot(qs[l].astype(BF16), kbs[l], _NT) * decay[l]).astype(BF16) for l in heads]
    au_aw = [_dot(attn[l], sol[l]) for l in heads]
    q_eff = [(qs[l] * col(2, l) - au_aw[l][:, d:]).astype(BF16) for l in heads]
    k_dec = [(ks[l] * col(3, l)).astype(BF16) for l in heads]

    state = [state_ref[l] for l in heads]
    outs = [[] for _ in heads]
    for c in range(nck):
        r = slice(c * CHUNK, (c + 1) * CHUNK)
        ktuw = [_dot(k_dec[l][r], sol[l][r], _TN) for l in heads]
        res = [_dot(jnp.concatenate([q_eff[l][r], ktuw[l][:, d:].astype(BF16)], axis=0),
                    state[l].astype(BF16)) for l in heads]
        for l in heads:
            outs[l].append(res[l][:CHUNK] + au_aw[l][r, :d])
            cd = cols[c * CHUNK:c * CHUNK + 1, 4 * hb + l:4 * hb + l + 1]
            state[l] = state[l] * cd + ktuw[l][:, :d] - res[l][CHUNK:]
    for l in heads:
        state_ref[l] = state[l]
        o = jnp.concatenate(outs[l], axis=0)
        o = o * lax.rsqrt(jnp.mean(o * o, axis=-1, keepdims=True) + EPS) * gain
        o_ref[:, sls[l]] = (o * _silu(z_ref[:, sls[l]].astype(F32))).astype(o_ref.dtype)


def _gated_delta(qkv3, proj3, z_col_off, gates, norm_gain, width, hb=8):
    bsz, t, _ = qkv3.shape
    d = GDN_HEAD_DIM
    nh = width // d
    hb = min(hb, nh)
    nhb = nh // hb
    cw = hb * d
    i = np.arange(TILE)[:, None]
    j = np.arange(TILE)[None, :]
    same = (i // CHUNK) == (j // CHUNK)
    incl = jnp.asarray(same & (i >= j), F32)
    blk = jnp.asarray(same, BF16)
    ip = np.arange(CHUNK)[:, None]
    nstrict = jnp.asarray(-1.0 * (ip > (j % CHUNK)), F32)
    eye = jnp.asarray(ip == (j % CHUNK), F32)
    col = lambda a, off: pl.BlockSpec((None, TILE, cw), lambda b, h, s: (b, s, off // cw + h))
    const = pl.BlockSpec((TILE, TILE), lambda b, h, s: (0, 0))
    packed = pl.BlockSpec((CHUNK, TILE), lambda b, h, s: (0, 0))
    return pl.pallas_call(
        functools.partial(_gdn_body, hb=hb),
        grid=(bsz, nhb, t // TILE),
        in_specs=[col(qkv3, 0), col(qkv3, width), col(qkv3, 2 * width), col(proj3, z_col_off),
                  pl.BlockSpec((None, 8, hb, TILE), lambda b, h, s: (b, 0, h, s)),
                  const, packed, packed, const,
                  pl.BlockSpec((1, d), lambda b, h, s: (0, 0))],
        out_specs=pl.BlockSpec((None, TILE, cw), lambda b, h, s: (b, s, h)),
        out_shape=jax.ShapeDtypeStruct((bsz, t, width), BF16),
        scratch_shapes=[pltpu.VMEM((hb, d, d), F32)],
        compiler_params=_params("parallel", "parallel", "arbitrary"),
        name="gated_delta",
    )(qkv3, qkv3, qkv3, proj3, gates, incl, nstrict, eye, blk, norm_gain.reshape(1, d).astype(F32))


def kernel(x, ln1_gain, w_in, ret_norm_gain, gdn_conv_w, gdn_A_log, gdn_dt_bias, gdn_norm_gain,
           w_out, ln2_gain, w_up, w_down, final_gain):
    bsz, t, dm = x.shape
    depth = w_in.shape[0]
    nh = gdn_A_log.shape[1]
    gw = nh * GDN_HEAD_DIM
    rw = w_out.shape[1] - gw
    main_cols = 4 * rw + 4 * gw
    m = bsz * t
    xf = x.reshape(m, dm)
    for l in range(depth):
        h = _rmsnorm(xf, ln1_gain[l], BF16)
        w_i = w_in[l].astype(BF16)
        proj = _matmul(h, w_i, BF16, n=main_cols, name="in_proj")
        proj3 = proj.reshape(bsz, t, main_cols)
        gates = _gdn_gates(h.reshape(bsz, t, dm), w_i[:, main_cols:].T, gdn_A_log[l], gdn_dt_bias[l])
        y_ret = _retention(proj3, ret_norm_gain[l], rw)
        qkv = _gdn_prep(proj3, gdn_conv_w[l], 4 * rw, gw)
        y_gdn = _gated_delta(qkv, proj3, 4 * rw + 3 * gw, gates, gdn_norm_gain[l], gw)
        xf = _out_proj(y_ret.reshape(m, rw), y_gdn.reshape(m, gw), w_out[l].astype(BF16), xf)
        h2 = _rmsnorm(xf, ln2_gain[l], BF16)
        act = _matmul(h2, w_up[l].astype(BF16), BF16, relu2=True, name="up_proj")
        xf = _down_proj(act, w_down[l].astype(BF16), xf)
    return _rmsnorm(xf, final_gain, F32).reshape(bsz, t, dm)
```

```python
import functools

import numpy as np
import jax
import jax.numpy as jnp
from jax import lax
from jax.experimental import pallas as pl
from jax.experimental.pallas import tpu as pltpu

F32 = jnp.float32
BF16 = jnp.bfloat16

CHUNK = 64
RET_HEADS = 8
GDN_HEAD_DIM = 128
CONV_WIDTH = 4
ROPE_BASE = 10000.0
EPS = 1e-6

V7X_VMEM_LIMIT_BYTES = 56 * 1024 * 1024
LANES = 128
TILE = 256
NEG_BIG = -1e30

_NT = (((1,), (1,)), ((), ()))
_TN = (((0,), (0,)), ((), ()))


def _params(*sem):
    return pltpu.CompilerParams(dimension_semantics=sem, vmem_limit_bytes=V7X_VMEM_LIMIT_BYTES)


def _dot(a, b, dims=None):
    if dims is None:
        return jnp.dot(a, b, preferred_element_type=F32)
    return lax.dot_general(a, b, dims, preferred_element_type=F32)


def _silu(x):
    return x * jax.nn.sigmoid(x)


def _rmsnorm_body(x_ref, g_ref, o_ref):
    x = x_ref[...].astype(F32)
    ms = jnp.mean(x * x, axis=-1, keepdims=True)
    o_ref[...] = (x * lax.rsqrt(ms + EPS) * g_ref[...]).astype(o_ref.dtype)


def _rmsnorm(x, gain, out_dtype, tm=512):
    m, d = x.shape
    tm = min(tm, m)
    return pl.pallas_call(
        _rmsnorm_body,
        grid=(m // tm,),
        in_specs=[pl.BlockSpec((tm, d), lambda i: (i, 0)),
                  pl.BlockSpec((1, d), lambda i: (0, 0))],
        out_specs=pl.BlockSpec((tm, d), lambda i: (i, 0)),
        out_shape=jax.ShapeDtypeStruct((m, d), out_dtype),
        compiler_params=_params("parallel"),
        name="rmsnorm",
    )(x, gain.reshape(1, d).astype(F32))


def _cast_specs(casts, n_steps, nj):
    specs = []
    for w in casts:
        rows, cols = w.shape
        assert rows % (16 * n_steps) == 0
        specs.append(pl.BlockSpec((rows // n_steps, cols), lambda i, j: (i * nj + j, 0)))
    return specs


def _mm_body(*refs, relu2, n_cast, has_ss, d_norm):
    a_ref, b_ref = refs[0], refs[1]
    pos = 2
    ss_ref = None
    if has_ss:
        ss_ref, pos = refs[pos], pos + 1
    cast_in = refs[pos:pos + n_cast]
    o_ref = refs[pos + n_cast]
    cast_out = refs[pos + n_cast + 1:]
    acc = _dot(a_ref[...], b_ref[...])
    if has_ss:
        acc = acc * lax.rsqrt(ss_ref[:, 0:1] * (1.0 / d_norm) + EPS)
    if relu2:
        acc = jnp.square(jnp.maximum(acc, 0.0))
    o_ref[...] = acc.astype(o_ref.dtype)
    for src, dst in zip(cast_in, cast_out):
        dst[...] = src[...].astype(dst.dtype)


def _matmul(a, b, out_dtype, *, n=None, relu2=False, ss=None, casts=(), tm=1024, tn=1024, name="matmul"):
    m, k = a.shape
    n = b.shape[1] if n is None else n
    tm, tn = min(tm, m), min(tn, n)
    assert m % tm == 0 and n % tn == 0
    ni, nj = m // tm, n // tn
    in_specs = [pl.BlockSpec((tm, k), lambda i, j: (i, 0)),
                pl.BlockSpec((k, tn), lambda i, j: (0, j))]
    args = [a, b]
    if ss is not None:
        in_specs.append(pl.BlockSpec((tm, ss.shape[1]), lambda i, j: (i, 0)))
        args.append(ss)
    return pl.pallas_call(
        functools.partial(_mm_body, relu2=relu2, n_cast=len(casts), has_ss=ss is not None, d_norm=k),
        grid=(ni, nj),
        in_specs=in_specs + _cast_specs(casts, ni * nj, nj),
        out_specs=[pl.BlockSpec((tm, tn), lambda i, j: (i, j))] + _cast_specs(casts, ni * nj, nj),
        out_shape=[jax.ShapeDtypeStruct((m, n), out_dtype)]
        + [jax.ShapeDtypeStruct(w.shape, BF16) for w in casts],
        compiler_params=_params("parallel", "arbitrary"),
        name=name,
    )(*args, *casts)


def _out_proj_body(a1_ref, a2_ref, b1_ref, b2_ref, x_ref, g_ref, o_ref, xg_ref, ss_ref):
    x1 = x_ref[...] + (_dot(a1_ref[...], b1_ref[...]) + _dot(a2_ref[...], b2_ref[...]))
    o_ref[...] = x1
    xg_ref[...] = (x1 * g_ref[...]).astype(xg_ref.dtype)
    part = jnp.broadcast_to(jnp.sum(x1 * x1, axis=-1, keepdims=True), ss_ref.shape)

    @pl.when(pl.program_id(1) == 0)
    def _():
        ss_ref[...] = part

    @pl.when(pl.program_id(1) > 0)
    def _():
        ss_ref[...] += part


def _out_proj(y1, y2, w, x, gain, tm=1024, tn=512):
    m, k1 = y1.shape
    assert y2.shape[1] == k1 and w.shape[0] == 2 * k1
    n = w.shape[1]
    tm, tn = min(tm, m), min(tn, n)
    nj = n // tn
    tile = lambda: pl.BlockSpec((tm, tn), lambda i, j: (i, j))
    return pl.pallas_call(
        _out_proj_body,
        grid=(m // tm, nj),
        in_specs=[pl.BlockSpec((tm, k1), lambda i, j: (i, 0)),
                  pl.BlockSpec((tm, k1), lambda i, j: (i, 0)),
                  pl.BlockSpec((k1, tn), lambda i, j: (0, j)),
                  pl.BlockSpec((k1, tn), lambda i, j: (1, j)),
                  tile(),
                  pl.BlockSpec((1, tn), lambda i, j: (0, j))],
        out_specs=[tile(), tile(), pl.BlockSpec((tm, LANES), lambda i, j: (i, 0))],
        out_shape=[jax.ShapeDtypeStruct((m, n), F32), jax.ShapeDtypeStruct((m, n), BF16),
                   jax.ShapeDtypeStruct((m, LANES), F32)],
        compiler_params=_params("parallel", "arbitrary"),
        name="out_proj",
    )(y1, y2, w, w, x, gain.reshape(1, n).astype(F32))


def _down_body(a_ref, b_ref, x_ref, o_ref):
    @pl.when(pl.program_id(2) == 0)
    def _():
        o_ref[...] = x_ref[...]

    o_ref[...] += _dot(a_ref[...], b_ref[...])


def _down_proj(a, b, x, tm=1024, tn=1024, tk=4096):
    m, k = a.shape
    n = b.shape[1]
    tm, tn, tk = min(tm, m), min(tn, n), min(tk, k)
    return pl.pallas_call(
        _down_body,
        grid=(m // tm, n // tn, k // tk),
        in_specs=[pl.BlockSpec((tm, tk), lambda i, j, kk: (i, kk)),
                  pl.BlockSpec((tk, tn), lambda i, j, kk: (kk, j)),
                  pl.BlockSpec((tm, tn), lambda i, j, kk: (i, j))],
        out_specs=pl.BlockSpec((tm, tn), lambda i, j, kk: (i, j)),
        out_shape=jax.ShapeDtypeStruct((m, n), F32),
        compiler_params=_params("parallel", "parallel", "arbitrary"),
        name="down_proj",
    )(a, b, x)


def _split3(x):
    hi = x.astype(BF16)
    r1 = x - hi.astype(F32)
    mid = r1.astype(BF16)
    lo = (r1 - mid.astype(F32)).astype(BF16)
    return hi, mid, lo


def _gates_body(h_ref, wt_ref, alog_ref, dtb_ref, ucum_ref, ublk_ref, o_ref, *, nh):
    logits = _dot(wt_ref[...], h_ref[...], _NT)
    beta = jax.nn.sigmoid(logits[:nh])
    g = -jnp.exp(alog_ref[...]) * jax.nn.softplus(logits[nh:] + dtb_ref[...])
    parts = _split3(g)
    ucum, ublk = ucum_ref[...], ublk_ref[...]
    gc = (_dot(parts[0], ucum) + _dot(parts[1], ucum)) + _dot(parts[2], ucum)
    gl = (_dot(parts[0], ublk) + _dot(parts[1], ublk)) + _dot(parts[2], ublk)
    o_ref[0] = beta
    o_ref[1] = gc
    o_ref[2] = jnp.exp(gc)
    o_ref[3] = jnp.exp(gl - gc)
    o_ref[4] = jnp.exp(gl)
    zero = jnp.zeros_like(beta)
    o_ref[5] = zero
    o_ref[6] = zero
    o_ref[7] = zero


def _gdn_gates(h3, w_tail_t, a_log, dt_bias, tt=512):
    bsz, t, d = h3.shape
    nh = a_log.shape[0]
    tt = min(tt, t)
    pos = np.arange(tt)
    same = (pos[:, None] // CHUNK) == (pos[None, :] // CHUNK)
    ucum = jnp.asarray(same & (pos[:, None] <= pos[None, :]), BF16)
    ublk = jnp.asarray(same, BF16)
    return pl.pallas_call(
        functools.partial(_gates_body, nh=nh),
        grid=(bsz, t // tt),
        in_specs=[pl.BlockSpec((None, tt, d), lambda b, i: (b, i, 0)),
                  pl.BlockSpec((2 * nh, d), lambda b, i: (0, 0)),
                  pl.BlockSpec((nh, 1), lambda b, i: (0, 0)),
                  pl.BlockSpec((nh, 1), lambda b, i: (0, 0)),
                  pl.BlockSpec((tt, tt), lambda b, i: (0, 0)),
                  pl.BlockSpec((tt, tt), lambda b, i: (0, 0))],
        out_specs=pl.BlockSpec((None, 8, nh, tt), lambda b, i: (b, 0, 0, i)),
        out_shape=jax.ShapeDtypeStruct((bsz, 8, nh, t), F32),
        compiler_params=_params("parallel", "parallel"),
        name="gdn_gates",
    )(h3, w_tail_t, a_log.reshape(nh, 1).astype(F32), dt_bias.reshape(nh, 1).astype(F32), ucum, ublk)


def _retention_tables(t, hd):
    half = hd // 2
    inv = ROPE_BASE ** (-np.arange(half, dtype=np.float64) * (2.0 / hd))
    ang = np.arange(t, dtype=np.float64)[:, None] * inv[None, :]
    lg = np.log1p(-np.exp2(-5.0 - np.arange(RET_HEADS, dtype=np.float64)))[:, None, None]
    i = np.arange(TILE)[:, None]
    j = np.arange(TILE)[None, :]
    dmask = np.exp(lg * np.abs(i - j)) * ((j // CHUNK) <= (i // CHUNK))
    xi = np.broadcast_to(np.exp(lg * (i + 1.0)), (RET_HEADS, TILE, hd))
    zeta = np.broadcast_to(np.exp(lg * (TILE - 1.0 - i)), (RET_HEADS, TILE, hd))
    gtile = np.broadcast_to(np.exp(lg * TILE), (RET_HEADS, 1, hd))
    f = lambda a: jnp.asarray(np.ascontiguousarray(a), F32)
    return f(np.cos(ang)), f(np.sin(ang)), f(dmask), f(xi), f(zeta), f(gtile)


def _ret_body(q_ref, k_ref, v_ref, g_ref, cos_ref, sin_ref, dm_ref, xi_ref, zeta_ref, gt_ref,
              gain_ref, o_ref, state_ref, *, hd, hb):
    @pl.when(pl.program_id(2) == 0)
    def _():
        state_ref[...] = jnp.zeros_like(state_ref)

    half = hd // 2
    cos, sin = cos_ref[...], sin_ref[...]

    def rope(x):
        x1, x2 = x[:, :half], x[:, half:]
        return jnp.concatenate([x1 * cos - x2 * sin, x1 * sin + x2 * cos], axis=-1)

    heads = range(hb)
    sls = [slice(l * hd, (l + 1) * hd) for l in heads]
    qb = [rope(q_ref[:, sls[l]].astype(F32)).astype(BF16) for l in heads]
    k = [rope(k_ref[:, sls[l]].astype(F32)) * (hd ** -0.5) for l in heads]
    v = [v_ref[:, sls[l]].astype(BF16) for l in heads]
    scores = [(_dot(qb[l], k[l].astype(BF16), _NT) * dm_ref[l]).astype(BF16) for l in heads]
    state = [state_ref[l] for l in heads]
    o = [_dot(scores[l], v[l]) + xi_ref[l] * _dot(qb[l], state[l].astype(BF16)) for l in heads]
    for l in heads:
        k_dec = (k[l] * zeta_ref[l]).astype(BF16)
        state_ref[l] = state[l] * gt_ref[l] + _dot(k_dec, v[l], _TN)
    for l in heads:
        mu = jnp.mean(o[l], axis=-1, keepdims=True)
        oc = o[l] - mu
        var = jnp.mean(oc * oc, axis=-1, keepdims=True)
        y = oc * lax.rsqrt(var + EPS) * gain_ref[:, sls[l]]
        o_ref[:, sls[l]] = (y * _silu(g_ref[:, sls[l]].astype(F32))).astype(o_ref.dtype)


def _retention(proj3, ret_gain, ret_width, hb=8):
    bsz, t, _ = proj3.shape
    hd = ret_width // RET_HEADS
    nhb = RET_HEADS // hb
    cw = hb * hd
    cos, sin, dmask, xi, zeta, gtile = _retention_tables(t, hd)
    col = lambda sec: pl.BlockSpec((None, TILE, cw), lambda b, h, i: (b, i, sec * nhb + h))
    head = lambda r, c: pl.BlockSpec((hb, r, c), lambda b, h, i: (h, 0, 0))
    return pl.pallas_call(
        functools.partial(_ret_body, hd=hd, hb=hb),
        grid=(bsz, nhb, t // TILE),
        in_specs=[col(0), col(1), col(2), col(3),
                  pl.BlockSpec((TILE, hd // 2), lambda b, h, i: (i, 0)),
                  pl.BlockSpec((TILE, hd // 2), lambda b, h, i: (i, 0)),
                  head(TILE, TILE), head(TILE, hd), head(TILE, hd), head(1, hd),
                  pl.BlockSpec((1, cw), lambda b, h, i: (0, h))],
        out_specs=pl.BlockSpec((None, TILE, cw), lambda b, h, i: (b, i, h)),
        out_shape=jax.ShapeDtypeStruct((bsz, t, ret_width), BF16),
        scratch_shapes=[pltpu.VMEM((hb, hd, hd), F32)],
        compiler_params=_params("parallel", "parallel", "arbitrary"),
        name="retention",
    )(proj3, proj3, proj3, proj3, cos, sin, dmask, xi, zeta, gtile,
      ret_gain.reshape(1, ret_width).astype(F32))


def _prep_body(x_ref, w_ref, o_ref, xbuf_ref, *, tt, n_qk_blocks, n_q_blocks):
    halo = 8

    @pl.when(pl.program_id(2) == 0)
    def _():
        xbuf_ref[0:halo, :] = jnp.zeros((halo, xbuf_ref.shape[1]), F32)

    x = x_ref[...].astype(F32)
    xbuf_ref[halo:halo + tt, :] = x
    w = w_ref[...]
    acc = w[CONV_WIDTH - 1:CONV_WIDTH, :] * x
    for j in range(CONV_WIDTH - 1):
        acc = acc + w[j:j + 1, :] * xbuf_ref[pl.ds(halo - (CONV_WIDTH - 1) + j, tt), :]
    xbuf_ref[0:halo, :] = x[tt - halo:tt, :]
    y = _silu(acc)
    c = pl.program_id(1)

    @pl.when(c < n_qk_blocks)
    def _():
        scale = jnp.where(c < n_q_blocks, GDN_HEAD_DIM ** -0.5, 1.0).astype(F32)
        for s in range(y.shape[1] // GDN_HEAD_DIM):
            sl = slice(s * GDN_HEAD_DIM, (s + 1) * GDN_HEAD_DIM)
            yh = y[:, sl]
            ss = jnp.sum(yh * yh, axis=-1, keepdims=True)
            o_ref[:, sl] = (yh * (lax.rsqrt(ss + EPS) * scale)).astype(o_ref.dtype)

    @pl.when(c >= n_qk_blocks)
    def _():
        o_ref[...] = y.astype(o_ref.dtype)


def _gdn_prep(proj3, conv_w, col_off, width, tt=512, cb=512):
    bsz, t, _ = proj3.shape
    tt = min(tt, t)
    ncb = 3 * width // cb
    return pl.pallas_call(
        functools.partial(_prep_body, tt=tt, n_qk_blocks=2 * width // cb, n_q_blocks=width // cb),
        grid=(bsz, ncb, t // tt),
        in_specs=[pl.BlockSpec((None, tt, cb), lambda b, c, i: (b, i, col_off // cb + c)),
                  pl.BlockSpec((CONV_WIDTH, cb), lambda b, c, i: (0, c))],
        out_specs=pl.BlockSpec((None, tt, cb), lambda b, c, i: (b, i, c)),
        out_shape=jax.ShapeDtypeStruct((bsz, t, 3 * width), BF16),
        scratch_shapes=[pltpu.VMEM((tt + 8, cb), F32)],
        compiler_params=_params("parallel", "parallel", "arbitrary"),
        name="gdn_prep",
    )(proj3, conv_w.astype(F32))


def _gdn_body(q_ref, k_ref, v_ref, z_ref, gate_ref, incl_ref, nstrict_ref, eye_ref, blk_ref, gain_ref,
              o_ref, state_ref, *, hb):
    d = GDN_HEAD_DIM
    nck = TILE // CHUNK

    @pl.when(pl.program_id(2) == 0)
    def _():
        state_ref[...] = jnp.zeros_like(state_ref)

    rows = jnp.concatenate([gate_ref[c] for c in range(5)]
                           + [jnp.zeros((LANES - 5 * hb, TILE), F32)], axis=0)
    cols = rows.T
    incl = incl_ref[...] > 0.0
    nstrict = nstrict_ref[...]
    eye = eye_ref[...]
    blk = blk_ref[...]
    gain = gain_ref[...]

    def block_diag(packed):
        return jnp.concatenate([packed] * nck, axis=0) * blk

    heads = range(hb)
    sls = [slice(l * d, (l + 1) * d) for l in heads]
    col = lambda c, l: cols[:, c * hb + l:c * hb + l + 1]
    kbs = [k_ref[:, sls[l]] for l in heads]
    ks = [kb.astype(F32) for kb in kbs]
    decay = [jnp.exp(jnp.where(incl, col(1, l) - gate_ref[1, l:l + 1, :], NEG_BIG)) for l in heads]
    k_beta = [ks[l] * col(0, l) for l in heads]
    kkd = [_dot(k_beta[l].astype(BF16), kbs[l], _NT) * decay[l] for l in heads]
    p = [sum(kkd[l][c * CHUNK:(c + 1) * CHUNK] for c in range(nck)) * nstrict for l in heads]
    acc = [eye + p[l] for l in heads]
    pw = [p[l].astype(BF16) for l in heads]
    pw = [_dot(pw[l], block_diag(pw[l])).astype(BF16) for l in heads]
    for _ in range(4):
        res = [_dot(jnp.concatenate([acc[l].astype(BF16), pw[l]], axis=0), block_diag(pw[l])) for l in heads]
        acc = [acc[l] + res[l][:CHUNK] for l in heads]
        pw = [res[l][CHUNK:].astype(BF16) for l in heads]
    acc = [acc[l] + _dot(acc[l].astype(BF16), block_diag(pw[l])) for l in heads]
    rhs = [jnp.concatenate([v_ref[:, sls[l]].astype(F32) * col(0, l), k_beta[l] * col(2, l)],
                           axis=-1).astype(BF16) for l in heads]
    sol = [_dot(block_diag(acc[l].astype(BF16)), rhs[l]).astype(BF16) for l in heads]
    qs = [q_ref[:, sls[l]].astype(F32) for l in heads]
    attn = [(_dot(qs[l].astype(BF16), kbs[l], _NT) * decay[l]).astype(BF16) for l in heads]
    au_aw = [_dot(attn[l], sol[l]) for l in heads]
    q_eff = [(qs[l] * col(2, l) - au_aw[l][:, d:]).astype(BF16) for l in heads]
    k_dec = [(ks[l] * col(3, l)).astype(BF16) for l in heads]

    state = [state_ref[l] for l in heads]
    outs = [[] for _ in heads]
    for c in range(nck):
        r = slice(c * CHUNK, (c + 1) * CHUNK)
        ktuw = [_dot(k_dec[l][r], sol[l][r], _TN) for l in heads]
        res = [_dot(jnp.concatenate([q_eff[l][r], ktuw[l][:, d:].astype(BF16)], axis=0),
                    state[l].astype(BF16)) for l in heads]
        for l in heads:
            outs[l].append(res[l][:CHUNK] + au_aw[l][r, :d])
            cd = cols[c * CHUNK:c * CHUNK + 1, 4 * hb + l:4 * hb + l + 1]
            state[l] = state[l] * cd + ktuw[l][:, :d] - res[l][CHUNK:]
    for l in heads:
        state_ref[l] = state[l]
        o = jnp.concatenate(outs[l], axis=0)
        o = o * lax.rsqrt(jnp.mean(o * o, axis=-1, keepdims=True) + EPS) * gain
        o_ref[:, sls[l]] = (o * _silu(z_ref[:, sls[l]].astype(F32))).astype(o_ref.dtype)


def _gated_delta(qkv3, proj3, z_col_off, gates, norm_gain, width, hb=8):
    bsz, t, _ = qkv3.shape
    d = GDN_HEAD_DIM
    nh = width // d
    hb = min(hb, nh)
    nhb = nh // hb
    cw = hb * d
    i = np.arange(TILE)[:, None]
    j = np.arange(TILE)[None, :]
    same = (i // CHUNK) == (j // CHUNK)
    incl = jnp.asarray(same & (i >= j), F32)
    blk = jnp.asarray(same, BF16)
    ip = np.arange(CHUNK)[:, None]
    nstrict = jnp.asarray(-1.0 * (ip > (j % CHUNK)), F32)
    eye = jnp.asarray(ip == (j % CHUNK), F32)
    col = lambda a, off: pl.BlockSpec((None, TILE, cw), lambda b, h, s: (b, s, off // cw + h))
    const = pl.BlockSpec((TILE, TILE), lambda b, h, s: (0, 0))
    packed = pl.BlockSpec((CHUNK, TILE), lambda b, h, s: (0, 0))
    return pl.pallas_call(
        functools.partial(_gdn_body, hb=hb),
        grid=(bsz, nhb, t // TILE),
        in_specs=[col(qkv3, 0), col(qkv3, width), col(qkv3, 2 * width), col(proj3, z_col_off),
                  pl.BlockSpec((None, 8, hb, TILE), lambda b, h, s: (b, 0, h, s)),
                  const, packed, packed, const,
                  pl.BlockSpec((1, d), lambda b, h, s: (0, 0))],
        out_specs=pl.BlockSpec((None, TILE, cw), lambda b, h, s: (b, s, h)),
        out_shape=jax.ShapeDtypeStruct((bsz, t, width), BF16),
        scratch_shapes=[pltpu.VMEM((hb, d, d), F32)],
        compiler_params=_params("parallel", "parallel", "arbitrary"),
        name="gated_delta",
    )(qkv3, qkv3, qkv3, proj3, gates, incl, nstrict, eye, blk, norm_gain.reshape(1, d).astype(F32))


def kernel(x, ln1_gain, w_in, ret_norm_gain, gdn_conv_w, gdn_A_log, gdn_dt_bias, gdn_norm_gain,
           w_out, ln2_gain, w_up, w_down, final_gain):
    bsz, t, dm = x.shape
    depth = w_in.shape[0]
    nh = gdn_A_log.shape[1]
    gw = nh * GDN_HEAD_DIM
    rw = w_out.shape[1] - gw
    main_cols = 4 * rw + 4 * gw
    m = bsz * t
    xf = x.reshape(m, dm)
    for l in range(depth):
        h = _rmsnorm(xf, ln1_gain[l], BF16)
        proj, w_u, w_o = _matmul(h, w_in[l].astype(BF16), BF16, n=main_cols,
                                 casts=(w_up[l], w_out[l]), name="in_proj")
        proj3 = proj.reshape(bsz, t, main_cols)
        gates = _gdn_gates(h.reshape(bsz, t, dm), w_in[l, :, main_cols:].T.astype(BF16),
                           gdn_A_log[l], gdn_dt_bias[l])
        y_ret = _retention(proj3, ret_norm_gain[l], rw)
        qkv = _gdn_prep(proj3, gdn_conv_w[l], 4 * rw, gw)
        y_gdn = _gated_delta(qkv, proj3, 4 * rw + 3 * gw, gates, gdn_norm_gain[l], gw)
        xf, xg, ss = _out_proj(y_ret.reshape(m, rw), y_gdn.reshape(m, gw), w_o, xf, ln2_gain[l])
        act, w_d = _matmul(xg, w_u, BF16, relu2=True, ss=ss, casts=(w_down[l],), name="up_proj")
        xf = _down_proj(act, w_d, xf)
    return _rmsnorm(xf, final_gain, F32).reshape(bsz, t, dm)
```

```python
import functools

import numpy as np
import jax
import jax.numpy as jnp
from jax import lax
from jax.experimental import pallas as pl
from jax.experimental.pallas import tpu as pltpu

F32 = jnp.float32
BF16 = jnp.bfloat16

CHUNK = 64
RET_HEADS = 8
GDN_HEAD_DIM = 128
CONV_WIDTH = 4
ROPE_BASE = 10000.0
EPS = 1e-6

V7X_VMEM_LIMIT_BYTES = 56 * 1024 * 1024
LANES = 128
TILE = 256
NEG_BIG = -1e30

_NT = (((1,), (1,)), ((), ()))
_TN = (((0,), (0,)), ((), ()))


def _params(*sem):
    return pltpu.CompilerParams(dimension_semantics=sem, vmem_limit_bytes=V7X_VMEM_LIMIT_BYTES)


def _dot(a, b, dims=None):
    if dims is None:
        return jnp.dot(a, b, preferred_element_type=F32)
    return lax.dot_general(a, b, dims, preferred_element_type=F32)


def _silu(x):
    return x * jax.nn.sigmoid(x)


def _rmsnorm_body(x_ref, g_ref, o_ref):
    x = x_ref[...].astype(F32)
    ms = jnp.mean(x * x, axis=-1, keepdims=True)
    o_ref[...] = (x * lax.rsqrt(ms + EPS) * g_ref[...]).astype(o_ref.dtype)


def _rmsnorm(x, gain, out_dtype, tm=512):
    m, d = x.shape
    tm = min(tm, m)
    return pl.pallas_call(
        _rmsnorm_body,
        grid=(m // tm,),
        in_specs=[pl.BlockSpec((tm, d), lambda i: (i, 0)),
                  pl.BlockSpec((1, d), lambda i: (0, 0))],
        out_specs=pl.BlockSpec((tm, d), lambda i: (i, 0)),
        out_shape=jax.ShapeDtypeStruct((m, d), out_dtype),
        compiler_params=_params("parallel"),
        name="rmsnorm",
    )(x, gain.reshape(1, d).astype(F32))


def _cast_specs(casts, n_steps, nj):
    specs = []
    for w in casts:
        rows, cols = w.shape
        assert rows % (16 * n_steps) == 0
        specs.append(pl.BlockSpec((rows // n_steps, cols),
                                  lambda i, j: (jnp.minimum(i * nj + j, n_steps - 1), 0)))
    return specs


def _in_first_body(a_ref, bt_ref, tail_ref, o_ref, wb_ref, wt_ref):
    wb = bt_ref[...].T.astype(wb_ref.dtype)
    wb_ref[...] = wb
    o_ref[...] = _dot(a_ref[...], wb).astype(o_ref.dtype)

    @pl.when(pl.program_id(0) == 0)
    def _():
        wt_ref[...] = tail_ref[...].astype(wt_ref.dtype)


def _in_proj_first(a, wt, n, out_dtype, tm=1024, tn=512):
    m, k = a.shape
    tm, tn = min(tm, m), min(tn, n)
    assert m % tm == 0 and n % tn == 0 and n % LANES == 0 and n < wt.shape[0] <= n + LANES
    return pl.pallas_call(
        _in_first_body,
        grid=(n // tn,),
        in_specs=[pl.BlockSpec((tm, k), lambda j: (0, 0)),
                  pl.BlockSpec((tn, k), lambda j: (j, 0)),
                  pl.BlockSpec((LANES, k), lambda j: (n // LANES, 0))],
        out_specs=[pl.BlockSpec((tm, tn), lambda j: (0, j)),
                   pl.BlockSpec((k, tn), lambda j: (0, j)),
                   pl.BlockSpec((LANES, k), lambda j: (0, 0))],
        out_shape=[jax.ShapeDtypeStruct((m, n), out_dtype), jax.ShapeDtypeStruct((k, n), BF16),
                   jax.ShapeDtypeStruct((LANES, k), BF16)],
        compiler_params=_params("arbitrary"),
        name="in_proj_first",
    )(a, wt, wt)


def _mm_body(*refs, relu2, n_cast, has_ss, has_alias, d_norm):
    a_ref, b_ref = refs[0], refs[1]
    pos = 2
    ss_ref = None
    if has_ss:
        ss_ref, pos = refs[pos], pos + 1
    if has_alias:
        pos += 1
    cast_in = refs[pos:pos + n_cast]
    o_ref = refs[pos + n_cast]
    cast_out = refs[pos + n_cast + 1:]
    acc = _dot(a_ref[...], b_ref[...])
    if has_ss:
        acc = acc * lax.rsqrt(ss_ref[:, 0:1] * (1.0 / d_norm) + EPS)
    if relu2:
        acc = jnp.square(jnp.maximum(acc, 0.0))
    o_ref[...] = acc.astype(o_ref.dtype)
    for src, dst in zip(cast_in, cast_out):
        dst[...] = src[...].astype(dst.dtype)


def _matmul(a, b, out_dtype, *, n=None, relu2=False, ss=None, casts=(), into=None,
            tm=1024, tn=1024, name="matmul"):
    m, k = a.shape
    n = b.shape[1] if n is None else n
    tm, tn = min(tm, m), min(tn, n)
    assert m % tm == 0 and n % tn == 0
    i0 = 0 if into is None else 1
    ni, nj = m // tm - i0, n // tn
    cast_steps = 1 << ((ni * nj).bit_length() - 1)
    in_specs = [pl.BlockSpec((tm, k), lambda i, j: (i + i0, 0)),
                pl.BlockSpec((k, tn), lambda i, j: (0, j))]
    args = [a, b]
    if ss is not None:
        in_specs.append(pl.BlockSpec((tm, ss.shape[1]), lambda i, j: (i + i0, 0)))
        args.append(ss)
    aliases = {}
    if into is not None:
        assert into.shape == (m, n) and into.dtype == out_dtype
        aliases = {len(args): 0}
        in_specs.append(pl.BlockSpec(memory_space=pl.ANY))
        args.append(into)
    return pl.pallas_call(
        functools.partial(_mm_body, relu2=relu2, n_cast=len(casts), has_ss=ss is not None,
                          has_alias=into is not None, d_norm=k),
        grid=(ni, nj),
        in_specs=in_specs + _cast_specs(casts, cast_steps, nj),
        out_specs=[pl.BlockSpec((tm, tn), lambda i, j: (i + i0, j))] + _cast_specs(casts, cast_steps, nj),
        out_shape=[jax.ShapeDtypeStruct((m, n), out_dtype)]
        + [jax.ShapeDtypeStruct(w.shape, BF16) for w in casts],
        input_output_aliases=aliases,
        compiler_params=_params("parallel", "arbitrary"),
        name=name,
    )(*args, *casts)


def _out_proj_body(a1_ref, a2_ref, b1_ref, b2_ref, x_ref, g_ref, o_ref, xg_ref, ss_ref):
    x1 = x_ref[...] + (_dot(a1_ref[...], b1_ref[...]) + _dot(a2_ref[...], b2_ref[...]))
    o_ref[...] = x1
    xg_ref[...] = (x1 * g_ref[...]).astype(xg_ref.dtype)
    part = jnp.broadcast_to(jnp.sum(x1 * x1, axis=-1, keepdims=True), ss_ref.shape)

    @pl.when(pl.program_id(1) == 0)
    def _():
        ss_ref[...] = part

    @pl.when(pl.program_id(1) > 0)
    def _():
        ss_ref[...] += part


def _out_proj(y1, y2, w, x, gain, tm=1024, tn=512):
    m, k1 = y1.shape
    assert y2.shape[1] == k1 and w.shape[0] == 2 * k1
    n = w.shape[1]
    tm, tn = min(tm, m), min(tn, n)
    nj = n // tn
    tile = lambda: pl.BlockSpec((tm, tn), lambda i, j: (i, j))
    return pl.pallas_call(
        _out_proj_body,
        grid=(m // tm, nj),
        in_specs=[pl.BlockSpec((tm, k1), lambda i, j: (i, 0)),
                  pl.BlockSpec((tm, k1), lambda i, j: (i, 0)),
                  pl.BlockSpec((k1, tn), lambda i, j: (0, j)),
                  pl.BlockSpec((k1, tn), lambda i, j: (1, j)),
                  tile(),
                  pl.BlockSpec((1, tn), lambda i, j: (0, j))],
        out_specs=[tile(), tile(), pl.BlockSpec((tm, LANES), lambda i, j: (i, 0))],
        out_shape=[jax.ShapeDtypeStruct((m, n), F32), jax.ShapeDtypeStruct((m, n), BF16),
                   jax.ShapeDtypeStruct((m, LANES), F32)],
        compiler_params=_params("parallel", "arbitrary"),
        name="out_proj",
    )(y1, y2, w, w, x, gain.reshape(1, n).astype(F32))


def _down_body(a_ref, b_ref, x_ref, o_ref):
    @pl.when(pl.program_id(2) == 0)
    def _():
        o_ref[...] = x_ref[...]

    o_ref[...] += _dot(a_ref[...], b_ref[...])


def _down_proj(a, b, x, tm=1024, tn=1024, tk=4096):
    m, k = a.shape
    n = b.shape[1]
    tm, tn, tk = min(tm, m), min(tn, n), min(tk, k)
    return pl.pallas_call(
        _down_body,
        grid=(m // tm, n // tn, k // tk),
        in_specs=[pl.BlockSpec((tm, tk), lambda i, j, kk: (i, kk)),
                  pl.BlockSpec((tk, tn), lambda i, j, kk: (kk, j)),
                  pl.BlockSpec((tm, tn), lambda i, j, kk: (i, j))],
        out_specs=pl.BlockSpec((tm, tn), lambda i, j, kk: (i, j)),
        out_shape=jax.ShapeDtypeStruct((m, n), F32),
        compiler_params=_params("parallel", "parallel", "arbitrary"),
        name="down_proj",
    )(a, b, x)


def _split3(x):
    hi = x.astype(BF16)
    r1 = x - hi.astype(F32)
    mid = r1.astype(BF16)
    lo = (r1 - mid.astype(F32)).astype(BF16)
    return hi, mid, lo


def _gates_body(h_ref, wt_ref, alog_ref, dtb_ref, ucum_ref, ublk_ref, o_ref, *, nh):
    logits = _dot(wt_ref[...], h_ref[...], _NT)
    beta = jax.nn.sigmoid(logits[:nh])
    g = -jnp.exp(alog_ref[...]) * jax.nn.softplus(logits[nh:2 * nh] + dtb_ref[...])
    parts = _split3(g)
    ucum, ublk = ucum_ref[...], ublk_ref[...]
    gc = (_dot(parts[0], ucum) + _dot(parts[1], ucum)) + _dot(parts[2], ucum)
    gl = (_dot(parts[0], ublk) + _dot(parts[1], ublk)) + _dot(parts[2], ublk)
    o_ref[0] = beta
    o_ref[1] = gc
    o_ref[2] = jnp.exp(gc)
    o_ref[3] = jnp.exp(gl - gc)
    o_ref[4] = jnp.exp(gl)
    zero = jnp.zeros_like(beta)
    o_ref[5] = zero
    o_ref[6] = zero
    o_ref[7] = zero


def _gdn_gates(h3, w_tail_t, a_log, dt_bias, tt=512):
    bsz, t, d = h3.shape
    nh = a_log.shape[0]
    tt = min(tt, t)
    pos = np.arange(tt)
    same = (pos[:, None] // CHUNK) == (pos[None, :] // CHUNK)
    ucum = jnp.asarray(same & (pos[:, None] <= pos[None, :]), BF16)
    ublk = jnp.asarray(same, BF16)
    return pl.pallas_call(
        functools.partial(_gates_body, nh=nh),
        grid=(bsz, t // tt),
        in_specs=[pl.BlockSpec((None, tt, d), lambda b, i: (b, i, 0)),
                  pl.BlockSpec((2 * nh, d), lambda b, i: (0, 0)),
                  pl.BlockSpec((nh, 1), lambda b, i: (0, 0)),
                  pl.BlockSpec((nh, 1), lambda b, i: (0, 0)),
                  pl.BlockSpec((tt, tt), lambda b, i: (0, 0)),
                  pl.BlockSpec((tt, tt), lambda b, i: (0, 0))],
        out_specs=pl.BlockSpec((None, 8, nh, tt), lambda b, i: (b, 0, 0, i)),
        out_shape=jax.ShapeDtypeStruct((bsz, 8, nh, t), F32),
        compiler_params=_params("parallel", "parallel"),
        name="gdn_gates",
    )(h3, w_tail_t, a_log.reshape(nh, 1).astype(F32), dt_bias.reshape(nh, 1).astype(F32), ucum, ublk)


def _retention_tables(t, hd):
    half = hd // 2
    inv = ROPE_BASE ** (-np.arange(half, dtype=np.float64) * (2.0 / hd))
    ang = np.arange(t, dtype=np.float64)[:, None] * inv[None, :]
    lg = np.log1p(-np.exp2(-5.0 - np.arange(RET_HEADS, dtype=np.float64)))[:, None, None]
    i = np.arange(TILE)[:, None]
    j = np.arange(TILE)[None, :]
    dmask = np.exp(lg * np.abs(i - j)) * ((j // CHUNK) <= (i // CHUNK))
    xi = np.broadcast_to(np.exp(lg * (i + 1.0)), (RET_HEADS, TILE, hd))
    zeta = np.broadcast_to(np.exp(lg * (TILE - 1.0 - i)), (RET_HEADS, TILE, hd))
    gtile = np.broadcast_to(np.exp(lg * TILE), (RET_HEADS, 1, hd))
    f = lambda a: jnp.asarray(np.ascontiguousarray(a), F32)
    return f(np.cos(ang)), f(np.sin(ang)), f(dmask), f(xi), f(zeta), f(gtile)


def _ret_body(q_ref, k_ref, v_ref, g_ref, cos_ref, sin_ref, dm_ref, xi_ref, zeta_ref, gt_ref,
              gain_ref, o_ref, state_ref, *, hd, hb):
    @pl.when(pl.program_id(2) == 0)
    def _():
        state_ref[...] = jnp.zeros_like(state_ref)

    half = hd // 2
    cos, sin = cos_ref[...], sin_ref[...]

    def rope(x):
        x1, x2 = x[:, :half], x[:, half:]
        return jnp.concatenate([x1 * cos - x2 * sin, x1 * sin + x2 * cos], axis=-1)

    heads = range(hb)
    sls = [slice(l * hd, (l + 1) * hd) for l in heads]
    qb = [rope(q_ref[:, sls[l]].astype(F32)).astype(BF16) for l in heads]
    k = [rope(k_ref[:, sls[l]].astype(F32)) * (hd ** -0.5) for l in heads]
    v = [v_ref[:, sls[l]].astype(BF16) for l in heads]
    scores = [(_dot(qb[l], k[l].astype(BF16), _NT) * dm_ref[l]).astype(BF16) for l in heads]
    state = [state_ref[l] for l in heads]
    o = [_dot(scores[l], v[l]) + xi_ref[l] * _dot(qb[l], state[l].astype(BF16)) for l in heads]
    for l in heads:
        k_dec = (k[l] * zeta_ref[l]).astype(BF16)
        state_ref[l] = state[l] * gt_ref[l] + _dot(k_dec, v[l], _TN)
    for l in heads:
        mu = jnp.mean(o[l], axis=-1, keepdims=True)
        oc = o[l] - mu
        var = jnp.mean(oc * oc, axis=-1, keepdims=True)
        y = oc * lax.rsqrt(var + EPS) * gain_ref[:, sls[l]]
        o_ref[:, sls[l]] = (y * _silu(g_ref[:, sls[l]].astype(F32))).astype(o_ref.dtype)


def _retention(proj3, ret_gain, ret_width, hb=8):
    bsz, t, _ = proj3.shape
    hd = ret_width // RET_HEADS
    nhb = RET_HEADS // hb
    cw = hb * hd
    cos, sin, dmask, xi, zeta, gtile = _retention_tables(t, hd)
    col = lambda sec: pl.BlockSpec((None, TILE, cw), lambda b, h, i: (b, i, sec * nhb + h))
    head = lambda r, c: pl.BlockSpec((hb, r, c), lambda b, h, i: (h, 0, 0))
    return pl.pallas_call(
        functools.partial(_ret_body, hd=hd, hb=hb),
        grid=(bsz, nhb, t // TILE),
        in_specs=[col(0), col(1), col(2), col(3),
                  pl.BlockSpec((TILE, hd // 2), lambda b, h, i: (i, 0)),
                  pl.BlockSpec((TILE, hd // 2), lambda b, h, i: (i, 0)),
                  head(TILE, TILE), head(TILE, hd), head(TILE, hd), head(1, hd),
                  pl.BlockSpec((1, cw), lambda b, h, i: (0, h))],
        out_specs=pl.BlockSpec((None, TILE, cw), lambda b, h, i: (b, i, h)),
        out_shape=jax.ShapeDtypeStruct((bsz, t, ret_width), BF16),
        scratch_shapes=[pltpu.VMEM((hb, hd, hd), F32)],
        compiler_params=_params("parallel", "parallel", "arbitrary"),
        name="retention",
    )(proj3, proj3, proj3, proj3, cos, sin, dmask, xi, zeta, gtile,
      ret_gain.reshape(1, ret_width).astype(F32))


def _prep_body(x_ref, w_ref, o_ref, xbuf_ref, *, tt, n_qk_blocks, n_q_blocks):
    halo = 8

    @pl.when(pl.program_id(2) == 0)
    def _():
        xbuf_ref[0:halo, :] = jnp.zeros((halo, xbuf_ref.shape[1]), F32)

    x = x_ref[...].astype(F32)
    xbuf_ref[halo:halo + tt, :] = x
    w = w_ref[...]
    acc = w[CONV_WIDTH - 1:CONV_WIDTH, :] * x
    for j in range(CONV_WIDTH - 1):
        acc = acc + w[j:j + 1, :] * xbuf_ref[pl.ds(halo - (CONV_WIDTH - 1) + j, tt), :]
    xbuf_ref[0:halo, :] = x[tt - halo:tt, :]
    y = _silu(acc)
    c = pl.program_id(1)

    @pl.when(c < n_qk_blocks)
    def _():
        scale = jnp.where(c < n_q_blocks, GDN_HEAD_DIM ** -0.5, 1.0).astype(F32)
        for s in range(y.shape[1] // GDN_HEAD_DIM):
            sl = slice(s * GDN_HEAD_DIM, (s + 1) * GDN_HEAD_DIM)
            yh = y[:, sl]
            ss = jnp.sum(yh * yh, axis=-1, keepdims=True)
            o_ref[:, sl] = (yh * (lax.rsqrt(ss + EPS) * scale)).astype(o_ref.dtype)

    @pl.when(c >= n_qk_blocks)
    def _():
        o_ref[...] = y.astype(o_ref.dtype)


def _gdn_prep(proj3, conv_w, col_off, width, tt=512, cb=512):
    bsz, t, _ = proj3.shape
    tt = min(tt, t)
    ncb = 3 * width // cb
    return pl.pallas_call(
        functools.partial(_prep_body, tt=tt, n_qk_blocks=2 * width // cb, n_q_blocks=width // cb),
        grid=(bsz, ncb, t // tt),
        in_specs=[pl.BlockSpec((None, tt, cb), lambda b, c, i: (b, i, col_off // cb + c)),
                  pl.BlockSpec((CONV_WIDTH, cb), lambda b, c, i: (0, c))],
        out_specs=pl.BlockSpec((None, tt, cb), lambda b, c, i: (b, i, c)),
        out_shape=jax.ShapeDtypeStruct((bsz, t, 3 * width), BF16),
        scratch_shapes=[pltpu.VMEM((tt + 8, cb), F32)],
        compiler_params=_params("parallel", "parallel", "arbitrary"),
        name="gdn_prep",
    )(proj3, conv_w.astype(F32))


def _gdn_body(q_ref, k_ref, v_ref, z_ref, gate_ref, incl_ref, nstrict_ref, eye_ref, blk_ref, gain_ref,
              o_ref, state_ref, *, hb):
    d = GDN_HEAD_DIM
    nck = TILE // CHUNK

    @pl.when(pl.program_id(2) == 0)
    def _():
        state_ref[...] = jnp.zeros_like(state_ref)

    rows = jnp.concatenate([gate_ref[c] for c in range(5)]
                           + [jnp.zeros((LANES - 5 * hb, TILE), F32)], axis=0)
    cols = rows.T
    incl = incl_ref[...] > 0.0
    nstrict = nstrict_ref[...]
    eye = eye_ref[...]
    blk = blk_ref[...]
    gain = gain_ref[...]

    def block_diag(packed):
        return jnp.concatenate([packed] * nck, axis=0) * blk

    heads = range(hb)
    sls = [slice(l * d, (l + 1) * d) for l in heads]
    col = lambda c, l: cols[:, c * hb + l:c * hb + l + 1]
    kbs = [k_ref[:, sls[l]] for l in heads]
    ks = [kb.astype(F32) for kb in kbs]
    decay = [jnp.exp(jnp.where(incl, col(1, l) - gate_ref[1, l:l + 1, :], NEG_BIG)) for l in heads]
    k_beta = [ks[l] * col(0, l) for l in heads]
    kkd = [_dot(k_beta[l].astype(BF16), kbs[l], _NT) * decay[l] for l in heads]
    p = [sum(kkd[l][c * CHUNK:(c + 1) * CHUNK] for c in range(nck)) * nstrict for l in heads]
    acc = [eye + p[l] for l in heads]
    pw = [p[l].astype(BF16) for l in heads]
    pw = [_dot(pw[l], block_diag(pw[l])).astype(BF16) for l in heads]
    for _ in range(4):
        res = [_dot(jnp.concatenate([acc[l].astype(BF16), pw[l]], axis=0), block_diag(pw[l])) for l in heads]
        acc = [acc[l] + res[l][:CHUNK] for l in heads]
        pw = [res[l][CHUNK:].astype(BF16) for l in heads]
    acc = [acc[l] + _dot(acc[l].astype(BF16), block_diag(pw[l])) for l in heads]
    rhs = [jnp.concatenate([v_ref[:, sls[l]].astype(F32) * col(0, l), k_beta[l] * col(2, l)],
                           axis=-1).astype(BF16) for l in heads]
    sol = [_dot(block_diag(acc[l].astype(BF16)), rhs[l]).astype(BF16) for l in heads]
    qs = [q_ref[:, sls[l]].astype(F32) for l in heads]
    attn = [(_dot(qs[l].astype(BF16), kbs[l], _NT) * decay[l]).astype(BF16) for l in heads]
    au_aw = [_dot(attn[l], sol[l]) for l in heads]
    q_eff = [(qs[l] * col(2, l) - au_aw[l][:, d:]).astype(BF16) for l in heads]
    k_dec = [(ks[l] * col(3, l)).astype(BF16) for l in heads]

    state = [state_ref[l] for l in heads]
    outs = [[] for _ in heads]
    for c in range(nck):
        r = slice(c * CHUNK, (c + 1) * CHUNK)
        ktuw = [_dot(k_dec[l][r], sol[l][r], _TN) for l in heads]
        res = [_dot(jnp.concatenate([q_eff[l][r], ktuw[l][:, d:].astype(BF16)], axis=0),
                    state[l].astype(BF16)) for l in heads]
        for l in heads:
            outs[l].append(res[l][:CHUNK] + au_aw[l][r, :d])
            cd = cols[c * CHUNK:c * CHUNK + 1, 4 * hb + l:4 * hb + l + 1]
            state[l] = state[l] * cd + ktuw[l][:, :d] - res[l][CHUNK:]
    for l in heads:
        state_ref[l] = state[l]
        o = jnp.concatenate(outs[l], axis=0)
        o = o * lax.rsqrt(jnp.mean(o * o, axis=-1, keepdims=True) + EPS) * gain
        o_ref[:, sls[l]] = (o * _silu(z_ref[:, sls[l]].astype(F32))).astype(o_ref.dtype)


def _gated_delta(qkv3, proj3, z_col_off, gates, norm_gain, width, hb=8):
    bsz, t, _ = qkv3.shape
    d = GDN_HEAD_DIM
    nh = width // d
    hb = min(hb, nh)
    nhb = nh // hb
    cw = hb * d
    i = np.arange(TILE)[:, None]
    j = np.arange(TILE)[None, :]
    same = (i // CHUNK) == (j // CHUNK)
    incl = jnp.asarray(same & (i >= j), F32)
    blk = jnp.asarray(same, BF16)
    ip = np.arange(CHUNK)[:, None]
    nstrict = jnp.asarray(-1.0 * (ip > (j % CHUNK)), F32)
    eye = jnp.asarray(ip == (j % CHUNK), F32)
    col = lambda a, off: pl.BlockSpec((None, TILE, cw), lambda b, h, s: (b, s, off // cw + h))
    const = pl.BlockSpec((TILE, TILE), lambda b, h, s: (0, 0))
    packed = pl.BlockSpec((CHUNK, TILE), lambda b, h, s: (0, 0))
    return pl.pallas_call(
        functools.partial(_gdn_body, hb=hb),
        grid=(bsz, nhb, t // TILE),
        in_specs=[col(qkv3, 0), col(qkv3, width), col(qkv3, 2 * width), col(proj3, z_col_off),
                  pl.BlockSpec((None, 8, hb, TILE), lambda b, h, s: (b, 0, h, s)),
                  const, packed, packed, const,
                  pl.BlockSpec((1, d), lambda b, h, s: (0, 0))],
        out_specs=pl.BlockSpec((None, TILE, cw), lambda b, h, s: (b, s, h)),
        out_shape=jax.ShapeDtypeStruct((bsz, t, width), BF16),
        scratch_shapes=[pltpu.VMEM((hb, d, d), F32)],
        compiler_params=_params("parallel", "parallel", "arbitrary"),
        name="gated_delta",
    )(qkv3, qkv3, qkv3, proj3, gates, incl, nstrict, eye, blk, norm_gain.reshape(1, d).astype(F32))


def kernel(x, ln1_gain, w_in, ret_norm_gain, gdn_conv_w, gdn_A_log, gdn_dt_bias, gdn_norm_gain,
           w_out, ln2_gain, w_up, w_down, final_gain):
    bsz, t, dm = x.shape
    depth = w_in.shape[0]
    nh = gdn_A_log.shape[1]
    gw = nh * GDN_HEAD_DIM
    rw = w_out.shape[1] - gw
    main_cols = 4 * rw + 4 * gw
    m = bsz * t
    xf = x.reshape(m, dm)
    for l in range(depth):
        h = _rmsnorm(xf, ln1_gain[l], BF16)
        proj, w_i, w_tail_t = _in_proj_first(h, w_in[l].T, main_cols, BF16)
        proj, w_u, w_o = _matmul(h, w_i, BF16, casts=(w_up[l], w_out[l]), into=proj, name="in_proj")
        proj3 = proj.reshape(bsz, t, main_cols)
        gates = _gdn_gates(h.reshape(bsz, t, dm), w_tail_t, gdn_A_log[l], gdn_dt_bias[l])
        y_ret = _retention(proj3, ret_norm_gain[l], rw)
        qkv = _gdn_prep(proj3, gdn_conv_w[l], 4 * rw, gw)
        y_gdn = _gated_delta(qkv, proj3, 4 * rw + 3 * gw, gates, gdn_norm_gain[l], gw)
        xf, xg, ss = _out_proj(y_ret.reshape(m, rw), y_gdn.reshape(m, gw), w_o, xf, ln2_gain[l])
        act, w_d = _matmul(xg, w_u, BF16, relu2=True, ss=ss, casts=(w_down[l],), name="up_proj")
        xf = _down_proj(act, w_d, xf)
    return _rmsnorm(xf, final_gain, F32).reshape(bsz, t, dm)
```

```python
import functools

import numpy as np
import jax
import jax.numpy as jnp
from jax import lax
from jax.experimental import pallas as pl
from jax.experimental.pallas import tpu as pltpu

F32 = jnp.float32
BF16 = jnp.bfloat16

CHUNK = 64
RET_HEADS = 8
GDN_HEAD_DIM = 128
CONV_WIDTH = 4
ROPE_BASE = 10000.0
EPS = 1e-6

V7X_VMEM_LIMIT_BYTES = 56 * 1024 * 1024
LANES = 128
TILE = 256
DOWN_TM = 1024
NEG_BIG = -1e30
LOG2E = 1.4426950408889634

_NT = (((1,), (1,)), ((), ()))
_TN = (((0,), (0,)), ((), ()))


def _params(*sem):
    return pltpu.CompilerParams(dimension_semantics=sem, vmem_limit_bytes=V7X_VMEM_LIMIT_BYTES)


def _dot(a, b, dims=None):
    if dims is None:
        return jnp.dot(a, b, preferred_element_type=F32)
    return lax.dot_general(a, b, dims, preferred_element_type=F32)


def _silu(x):
    return x * jax.nn.sigmoid(x)


def _rms(x, gain):
    return x * lax.rsqrt(jnp.mean(x * x, axis=-1, keepdims=True) + EPS) * gain


def _rmsnorm_body(x_ref, g_ref, *rest):
    o_ref = rest[-1]
    o_ref[...] = _rms(x_ref[...].astype(F32), g_ref[...]).astype(o_ref.dtype)


def _rmsnorm(x, gain, out_dtype, tm=512, into=None):
    m, d = x.shape
    tm = min(tm, m)
    in_specs = [pl.BlockSpec((tm, d), lambda i: (i, 0)),
                pl.BlockSpec((1, d), lambda i: (0, 0))]
    args = [x, gain.reshape(1, d).astype(F32)]
    off, out_shape, aliases = 0, (m, d), {}
    if into is not None:
        dst, row0 = into
        assert row0 % tm == 0 and dst.dtype == out_dtype and dst.shape[1] == d
        off, out_shape, aliases = row0 // tm, dst.shape, {2: 0}
        in_specs.append(pl.BlockSpec(memory_space=pl.ANY))
        args.append(dst)
    return pl.pallas_call(
        _rmsnorm_body,
        grid=(m // tm,),
        in_specs=in_specs,
        out_specs=pl.BlockSpec((tm, d), lambda i: (i + off, 0)),
        out_shape=jax.ShapeDtypeStruct(out_shape, out_dtype),
        input_output_aliases=aliases,
        compiler_params=_params("parallel"),
        name="rmsnorm",
    )(*args)


def _cast_specs(casts, n_steps, nj):
    specs = []
    for w in casts:
        rows, cols = w.shape
        assert rows % (16 * n_steps) == 0
        specs.append(pl.BlockSpec((rows // n_steps, cols),
                                  lambda i, j: (jnp.minimum(i * nj + j, n_steps - 1), 0)))
    return specs


def _in_first_body(a_ref, bt_ref, o_ref, wb_ref):
    wb = bt_ref[...].T.astype(wb_ref.dtype)
    wb_ref[...] = wb
    o_ref[...] = _dot(a_ref[...], wb).astype(o_ref.dtype)


def _in_proj_first(a, wt, n, out_dtype, tm=1024, tn=512):
    m, k = a.shape
    tm, tn = min(tm, m), min(tn, n)
    assert m % tm == 0 and n % tn == 0
    return pl.pallas_call(
        _in_first_body,
        grid=(n // tn,),
        in_specs=[pl.BlockSpec((tm, k), lambda j: (0, 0)),
                  pl.BlockSpec((tn, k), lambda j: (j, 0))],
        out_specs=[pl.BlockSpec((tm, tn), lambda j: (0, j)),
                   pl.BlockSpec((k, tn), lambda j: (0, j))],
        out_shape=[jax.ShapeDtypeStruct((m, n), out_dtype), jax.ShapeDtypeStruct((k, n), BF16)],
        compiler_params=_params("parallel"),
        name="in_proj_first",
    )(a, wt)


def _mm_body(*refs, relu2, n_cast, has_ss, has_alias, d_norm):
    a_ref, b_ref = refs[0], refs[1]
    pos = 2
    ss_ref = None
    if has_ss:
        ss_ref, pos = refs[pos], pos + 1
    if has_alias:
        pos += 1
    cast_in = refs[pos:pos + n_cast]
    o_ref = refs[pos + n_cast]
    cast_out = refs[pos + n_cast + 1:]
    acc = _dot(a_ref[...], b_ref[...])
    if has_ss:
        acc = acc * lax.rsqrt(ss_ref[:, 0:1] * (1.0 / d_norm) + EPS)
    if relu2:
        acc = jnp.square(jnp.maximum(acc, 0.0))
    o_ref[...] = acc.astype(o_ref.dtype)
    for src, dst in zip(cast_in, cast_out):
        dst[...] = src[...].astype(dst.dtype)


def _matmul(a, b, out_dtype, *, n=None, relu2=False, ss=None, casts=(), into=None,
            tm=1024, tn=1024, name="matmul"):
    m, k = a.shape
    n = b.shape[1] if n is None else n
    tm, tn = min(tm, m), min(tn, n)
    assert m % tm == 0 and n % tn == 0
    i0 = 0 if into is None else 1
    ni, nj = m // tm - i0, n // tn
    cast_steps = 1 << ((ni * nj).bit_length() - 1)
    in_specs = [pl.BlockSpec((tm, k), lambda i, j: (i + i0, 0)),
                pl.BlockSpec((k, tn), lambda i, j: (0, j))]
    args = [a, b]
    if ss is not None:
        in_specs.append(pl.BlockSpec((tm, ss.shape[1]), lambda i, j: (i + i0, 0)))
        args.append(ss)
    aliases = {}
    if into is not None:
        assert into.shape == (m, n) and into.dtype == out_dtype
        aliases = {len(args): 0}
        in_specs.append(pl.BlockSpec(memory_space=pl.ANY))
        args.append(into)
    return pl.pallas_call(
        functools.partial(_mm_body, relu2=relu2, n_cast=len(casts), has_ss=ss is not None,
                          has_alias=into is not None, d_norm=k),
        grid=(ni, nj),
        in_specs=in_specs + _cast_specs(casts, cast_steps, nj),
        out_specs=[pl.BlockSpec((tm, tn), lambda i, j: (i + i0, j))] + _cast_specs(casts, cast_steps, nj),
        out_shape=[jax.ShapeDtypeStruct((m, n), out_dtype)]
        + [jax.ShapeDtypeStruct(w.shape, BF16) for w in casts],
        input_output_aliases=aliases,
        compiler_params=_params("parallel", "arbitrary"),
        name=name,
    )(*args, *casts)


def _out_proj_body(a1_ref, a2_ref, b1_ref, b2_ref, x_ref, g_ref, o_ref, xg_ref, ss_ref):
    x1 = x_ref[...] + (_dot(a1_ref[...], b1_ref[...]) + _dot(a2_ref[...], b2_ref[...]))
    o_ref[...] = x1
    xg_ref[...] = (x1 * g_ref[...]).astype(xg_ref.dtype)
    part = jnp.broadcast_to(jnp.sum(x1 * x1, axis=-1, keepdims=True), ss_ref.shape)

    @pl.when(pl.program_id(1) == 0)
    def _():
        ss_ref[...] = part

    @pl.when(pl.program_id(1) > 0)
    def _():
        ss_ref[...] += part


def _out_proj(y1, y2, w, x, gain, tm=1024, tn=512):
    m, k1 = y1.shape
    assert y2.shape[1] == k1 and w.shape[0] == 2 * k1
    n = w.shape[1]
    tm, tn = min(tm, m), min(tn, n)
    nj = n // tn
    tile = lambda: pl.BlockSpec((tm, tn), lambda i, j: (i, j))
    return pl.pallas_call(
        _out_proj_body,
        grid=(m // tm, nj),
        in_specs=[pl.BlockSpec((tm, k1), lambda i, j: (i, 0)),
                  pl.BlockSpec((tm, k1), lambda i, j: (i, 0)),
                  pl.BlockSpec((k1, tn), lambda i, j: (0, j)),
                  pl.BlockSpec((k1, tn), lambda i, j: (1, j)),
                  tile(),
                  pl.BlockSpec((1, tn), lambda i, j: (0, j))],
        out_specs=[tile(), tile(), pl.BlockSpec((tm, LANES), lambda i, j: (i, 0))],
        out_shape=[jax.ShapeDtypeStruct((m, n), F32), jax.ShapeDtypeStruct((m, n), BF16),
                   jax.ShapeDtypeStruct((m, LANES), F32)],
        compiler_params=_params("parallel", "arbitrary"),
        name="out_proj",
    )(y1, y2, w, w, x, gain.reshape(1, n).astype(F32))


def _down_body(a_ref, b_ref, x_ref, *rest):
    o_ref = rest[-1] if len(rest) == 1 else rest[2]
    @pl.when(pl.program_id(2) == 0)
    def _():
        o_ref[...] = x_ref[...]

    o_ref[...] += _dot(a_ref[...], b_ref[...])
    if len(rest) > 1:
        src_ref, g_ref, _, dst_ref = rest
        dst_ref[...] = _rms(src_ref[...], g_ref[...])


def _down_proj(a, b, x, rows, *, side=None, tm=1024, tn=1024, tk=4096):
    m, k = a.shape
    n = b.shape[1]
    tm, tn, tk = min(tm, m), min(tn, n), min(tk, k)
    assert rows[0] % tm == 0 and rows[1] % tm == 0
    r0, ni, nj, nk = rows[0] // tm, (rows[1] - rows[0]) // tm, n // tn, k // tk
    in_specs = [pl.BlockSpec((tm, tk), lambda i, j, kk: (i + r0, kk)),
                pl.BlockSpec((tk, tn), lambda i, j, kk: (kk, j)),
                pl.BlockSpec((tm, tn), lambda i, j, kk: (i + r0, j))]
    out_specs = [pl.BlockSpec((tm, tn), lambda i, j, kk: (i, j))]
    out_shape = [jax.ShapeDtypeStruct((ni * tm, n), F32)]
    args = [a, b, x]
    if side is not None:
        src, gain, m_total = side
        steps = ni * nj * nk
        rs = src.shape[0] // steps
        assert src.shape[0] % steps == 0 and rs % 8 == 0 and src.shape[1] == n
        lin = lambda i, j, kk: ((i * nj + j) * nk + kk, 0)
        in_specs += [pl.BlockSpec((rs, n), lin), pl.BlockSpec((1, n), lambda i, j, kk: (0, 0))]
        out_specs.append(pl.BlockSpec((rs, n), lin))
        out_shape.append(jax.ShapeDtypeStruct((m_total, n), F32))
        args += [src, gain.reshape(1, n).astype(F32)]
    return pl.pallas_call(
        _down_body,
        grid=(ni, nj, nk),
        in_specs=in_specs,
        out_specs=out_specs,
        out_shape=out_shape,
        compiler_params=_params("parallel", "parallel", "arbitrary"),
        name="down_proj",
    )(*args)


def _split3(x):
    hi = x.astype(BF16)
    r1 = x - hi.astype(F32)
    mid = r1.astype(BF16)
    lo = (r1 - mid.astype(F32)).astype(BF16)
    return hi, mid, lo


def _norm_gates_body(x_ref, gain_ref, tail_ref, alog_ref, dtb_ref, ucum_ref, ublk_ref, h_ref, o_ref, *, nh):
    h = _rms(x_ref[...], gain_ref[...]).astype(h_ref.dtype)
    h_ref[...] = h
    logits = _dot(tail_ref[0:2 * nh, :].astype(h.dtype), h, _NT)
    beta = jax.nn.sigmoid(logits[:nh])
    g = -jnp.exp(alog_ref[...]) * jax.nn.softplus(logits[nh:] + dtb_ref[...])
    parts = _split3(g)
    ucum, ublk = ucum_ref[...], ublk_ref[...]
    gc = (_dot(parts[0], ucum) + _dot(parts[1], ucum)) + _dot(parts[2], ucum)
    gl = (_dot(parts[0], ublk) + _dot(parts[1], ublk)) + _dot(parts[2], ublk)
    o_ref[0] = beta
    o_ref[1] = gc
    o_ref[2] = jnp.exp(gc)
    o_ref[3] = jnp.exp(gl - gc)
    o_ref[4] = jnp.exp(gl)
    o_ref[5] = gc * LOG2E
    zero = jnp.zeros_like(beta)
    o_ref[6] = zero
    o_ref[7] = zero


def _norm_gates(x, gain, wt, n, a_log, dt_bias, t, tm=512):
    m, d = x.shape
    nh = a_log.shape[0]
    tm = min(tm, t)
    assert t % tm == 0 and tm % CHUNK == 0 and n % LANES == 0 and n + 2 * nh == wt.shape[0]
    nt = t // tm
    pos = np.arange(tm)
    same = (pos[:, None] // CHUNK) == (pos[None, :] // CHUNK)
    ucum = jnp.asarray(same & (pos[:, None] <= pos[None, :]), BF16)
    ublk = jnp.asarray(same, BF16)
    const = lambda shape: pl.BlockSpec(shape, lambda i: (0, 0))
    return pl.pallas_call(
        functools.partial(_norm_gates_body, nh=nh),
        grid=(m // tm,),
        in_specs=[pl.BlockSpec((tm, d), lambda i: (i, 0)),
                  const((1, d)),
                  pl.BlockSpec((LANES, d), lambda i: (n // LANES, 0)),
                  const((nh, 1)), const((nh, 1)), const((tm, tm)), const((tm, tm))],
        out_specs=[pl.BlockSpec((tm, d), lambda i: (i, 0)),
                   pl.BlockSpec((None, 8, nh, tm), lambda i: (i // nt, 0, 0, i % nt))],
        out_shape=[jax.ShapeDtypeStruct((m, d), BF16), jax.ShapeDtypeStruct((m // t, 8, nh, t), F32)],
        compiler_params=_params("parallel"),
        name="norm_gates",
    )(x, gain.reshape(1, d).astype(F32), wt, a_log.reshape(nh, 1).astype(F32),
      dt_bias.reshape(nh, 1).astype(F32), ucum, ublk)


def _retention_tables(t, hd):
    half = hd // 2
    inv = ROPE_BASE ** (-np.arange(half, dtype=np.float64) * (2.0 / hd))
    ang = np.arange(t, dtype=np.float64)[:, None] * inv[None, :]
    lg = np.log1p(-np.exp2(-5.0 - np.arange(RET_HEADS, dtype=np.float64)))[:, None, None]
    i = np.arange(TILE)[:, None]
    j = np.arange(TILE)[None, :]
    dmask = np.exp(lg * np.abs(i - j)) * ((j // CHUNK) <= (i // CHUNK))
    xi = np.broadcast_to(np.exp(lg * (i + 1.0)), (RET_HEADS, TILE, hd))
    zeta = np.broadcast_to(np.exp(lg * (TILE - 1.0 - i)), (RET_HEADS, TILE, hd))
    gtile = np.broadcast_to(np.exp(lg * TILE), (RET_HEADS, 1, hd))
    f = lambda a: jnp.asarray(np.ascontiguousarray(a), F32)
    return f(np.cos(ang)), f(np.sin(ang)), f(dmask), f(xi), f(zeta), f(gtile)


def _ret_body(q_ref, k_ref, v_ref, g_ref, cos_ref, sin_ref, dm_ref, xi_ref, zeta_ref, gt_ref,
              gain_ref, o_ref, state_ref, *, hd, hb):
    @pl.when(pl.program_id(2) == 0)
    def _():
        state_ref[...] = jnp.zeros_like(state_ref)

    half = hd // 2
    cos, sin = cos_ref[...], sin_ref[...]

    def rope(x):
        x1, x2 = x[:, :half], x[:, half:]
        return jnp.concatenate([x1 * cos - x2 * sin, x1 * sin + x2 * cos], axis=-1)

    heads = range(hb)
    sls = [slice(l * hd, (l + 1) * hd) for l in heads]
    qb = [rope(q_ref[:, sls[l]].astype(F32)).astype(BF16) for l in heads]
    k = [rope(k_ref[:, sls[l]].astype(F32)) * (hd ** -0.5) for l in heads]
    v = [v_ref[:, sls[l]].astype(BF16) for l in heads]
    scores = [(_dot(qb[l], k[l].astype(BF16), _NT) * dm_ref[l]).astype(BF16) for l in heads]
    state = [state_ref[l] for l in heads]
    o = [_dot(scores[l], v[l]) + xi_ref[l] * _dot(qb[l], state[l].astype(BF16)) for l in heads]
    for l in heads:
        k_dec = (k[l] * zeta_ref[l]).astype(BF16)
        state_ref[l] = state[l] * gt_ref[l] + _dot(k_dec, v[l], _TN)
    for l in heads:
        mu = jnp.mean(o[l], axis=-1, keepdims=True)
        oc = o[l] - mu
        var = jnp.mean(oc * oc, axis=-1, keepdims=True)
        y = oc * lax.rsqrt(var + EPS) * gain_ref[:, sls[l]]
        o_ref[:, sls[l]] = (y * _silu(g_ref[:, sls[l]].astype(F32))).astype(o_ref.dtype)


def _retention(proj3, ret_gain, ret_width, hb=8):
    bsz, t, _ = proj3.shape
    hd = ret_width // RET_HEADS
    nhb = RET_HEADS // hb
    cw = hb * hd
    cos, sin, dmask, xi, zeta, gtile = _retention_tables(t, hd)
    col = lambda sec: pl.BlockSpec((None, TILE, cw), lambda b, h, i: (b, i, sec * nhb + h))
    head = lambda r, c: pl.BlockSpec((hb, r, c), lambda b, h, i: (h, 0, 0))
    return pl.pallas_call(
        functools.partial(_ret_body, hd=hd, hb=hb),
        grid=(bsz, nhb, t // TILE),
        in_specs=[col(0), col(1), col(2), col(3),
                  pl.BlockSpec((TILE, hd // 2), lambda b, h, i: (i, 0)),
                  pl.BlockSpec((TILE, hd // 2), lambda b, h, i: (i, 0)),
                  head(TILE, TILE), head(TILE, hd), head(TILE, hd), head(1, hd),
                  pl.BlockSpec((1, cw), lambda b, h, i: (0, h))],
        out_specs=pl.BlockSpec((None, TILE, cw), lambda b, h, i: (b, i, h)),
        out_shape=jax.ShapeDtypeStruct((bsz, t, ret_width), BF16),
        scratch_shapes=[pltpu.VMEM((hb, hd, hd), F32)],
        compiler_params=_params("parallel", "parallel", "arbitrary"),
        name="retention",
    )(proj3, proj3, proj3, proj3, cos, sin, dmask, xi, zeta, gtile,
      ret_gain.reshape(1, ret_width).astype(F32))


def _prep_body(x_ref, w_ref, o_ref, xbuf_ref, *, tt, n_qk_blocks, n_q_blocks):
    halo = 8

    @pl.when(pl.program_id(2) == 0)
    def _():
        xbuf_ref[0:halo, :] = jnp.zeros((halo, xbuf_ref.shape[1]), F32)

    x = x_ref[...].astype(F32)
    xbuf_ref[halo:halo + tt, :] = x
    w = w_ref[...]
    acc = w[CONV_WIDTH - 1:CONV_WIDTH, :] * x
    for j in range(CONV_WIDTH - 1):
        acc = acc + w[j:j + 1, :] * xbuf_ref[pl.ds(halo - (CONV_WIDTH - 1) + j, tt), :]
    xbuf_ref[0:halo, :] = x[tt - halo:tt, :]
    y = _silu(acc)
    c = pl.program_id(1)

    @pl.when(c < n_qk_blocks)
    def _():
        scale = jnp.where(c < n_q_blocks, GDN_HEAD_DIM ** -0.5, 1.0).astype(F32)
        for s in range(y.shape[1] // GDN_HEAD_DIM):
            sl = slice(s * GDN_HEAD_DIM, (s + 1) * GDN_HEAD_DIM)
            yh = y[:, sl]
            ss = jnp.sum(yh * yh, axis=-1, keepdims=True)
            o_ref[:, sl] = (yh * (lax.rsqrt(ss + EPS) * scale)).astype(o_ref.dtype)

    @pl.when(c >= n_qk_blocks)
    def _():
        o_ref[...] = y.astype(o_ref.dtype)


def _gdn_prep(proj3, conv_w, col_off, width, tt=512, cb=512):
    bsz, t, _ = proj3.shape
    tt = min(tt, t)
    ncb = 3 * width // cb
    return pl.pallas_call(
        functools.partial(_prep_body, tt=tt, n_qk_blocks=2 * width // cb, n_q_blocks=width // cb),
        grid=(bsz, ncb, t // tt),
        in_specs=[pl.BlockSpec((None, tt, cb), lambda b, c, i: (b, i, col_off // cb + c)),
                  pl.BlockSpec((CONV_WIDTH, cb), lambda b, c, i: (0, c))],
        out_specs=pl.BlockSpec((None, tt, cb), lambda b, c, i: (b, i, c)),
        out_shape=jax.ShapeDtypeStruct((bsz, t, 3 * width), BF16),
        scratch_shapes=[pltpu.VMEM((tt + 8, cb), F32)],
        compiler_params=_params("parallel", "parallel", "arbitrary"),
        name="gdn_prep",
    )(proj3, conv_w.astype(F32))


def _gdn_body(q_ref, k_ref, v_ref, z_ref, gate_ref, incl_ref, nstrict_ref, eye_ref, blk_ref, gain_ref,
              o_ref, state_ref, *, hb):
    d = GDN_HEAD_DIM
    nck = TILE // CHUNK

    @pl.when(pl.program_id(2) == 0)
    def _():
        state_ref[...] = jnp.zeros_like(state_ref)

    rows = jnp.concatenate([gate_ref[c] for c in range(6)]
                           + [jnp.zeros((LANES - 6 * hb, TILE), F32)], axis=0)
    cols = rows.T
    incl = incl_ref[...] > 0.0
    nstrict = nstrict_ref[...]
    eye = eye_ref[...]
    blk = blk_ref[...]
    gain = gain_ref[...]

    def block_diag(packed):
        return jnp.concatenate([packed] * nck, axis=0) * blk

    heads = range(hb)
    sls = [slice(l * d, (l + 1) * d) for l in heads]
    col = lambda c, l: cols[:, c * hb + l:c * hb + l + 1]
    kbs = [k_ref[:, sls[l]] for l in heads]
    ks = [kb.astype(F32) for kb in kbs]
    decay = [jnp.exp2(jnp.where(incl, col(5, l) - gate_ref[5, l:l + 1, :], NEG_BIG)) for l in heads]
    k_beta = [ks[l] * col(0, l) for l in heads]
    kkd = [_dot(k_beta[l].astype(BF16), kbs[l], _NT) * decay[l] for l in heads]
    p = [sum(kkd[l][c * CHUNK:(c + 1) * CHUNK] for c in range(nck)) * nstrict for l in heads]
    acc = [eye + p[l] for l in heads]
    pw = [p[l].astype(BF16) for l in heads]
    pw = [_dot(pw[l], block_diag(pw[l])).astype(BF16) for l in heads]
    for _ in range(4):
        res = [_dot(jnp.concatenate([acc[l].astype(BF16), pw[l]], axis=0), block_diag(pw[l])) for l in heads]
        acc = [acc[l] + res[l][:CHUNK] for l in heads]
        pw = [res[l][CHUNK:].astype(BF16) for l in heads]
    acc = [acc[l] + _dot(acc[l].astype(BF16), block_diag(pw[l])) for l in heads]
    rhs = [jnp.concatenate([v_ref[:, sls[l]].astype(F32) * col(0, l), k_beta[l] * col(2, l)],
                           axis=-1).astype(BF16) for l in heads]
    sol = [_dot(block_diag(acc[l].astype(BF16)), rhs[l]).astype(BF16) for l in heads]
    qs = [q_ref[:, sls[l]].astype(F32) for l in heads]
    attn = [(_dot(qs[l].astype(BF16), kbs[l], _NT) * decay[l]).astype(BF16) for l in heads]
    au_aw = [_dot(attn[l], sol[l]) for l in heads]
    q_eff = [(qs[l] * col(2, l) - au_aw[l][:, d:]).astype(BF16) for l in heads]
    k_dec = [(ks[l] * col(3, l)).astype(BF16) for l in heads]

    state = [state_ref[l] for l in heads]
    outs = [[] for _ in heads]
    for c in range(nck):
        r = slice(c * CHUNK, (c + 1) * CHUNK)
        ktuw = [_dot(k_dec[l][r], sol[l][r], _TN) for l in heads]
        res = [_dot(jnp.concatenate([q_eff[l][r], ktuw[l][:, d:].astype(BF16)], axis=0),
                    state[l].astype(BF16)) for l in heads]
        for l in heads:
            outs[l].append(res[l][:CHUNK] + au_aw[l][r, :d])
            cd = cols[c * CHUNK:c * CHUNK + 1, 4 * hb + l:4 * hb + l + 1]
            state[l] = state[l] * cd + ktuw[l][:, :d] - res[l][CHUNK:]
    for l in heads:
        state_ref[l] = state[l]
        o = jnp.concatenate(outs[l], axis=0)
        o = o * lax.rsqrt(jnp.mean(o * o, axis=-1, keepdims=True) + EPS) * gain
        o_ref[:, sls[l]] = (o * _silu(z_ref[:, sls[l]].astype(F32))).astype(o_ref.dtype)


def _gated_delta(qkv3, proj3, z_col_off, gates, norm_gain, width, hb=8):
    bsz, t, _ = qkv3.shape
    d = GDN_HEAD_DIM
    nh = width // d
    hb = min(hb, nh)
    nhb = nh // hb
    cw = hb * d
    i = np.arange(TILE)[:, None]
    j = np.arange(TILE)[None, :]
    same = (i // CHUNK) == (j // CHUNK)
    incl = jnp.asarray(same & (i >= j), F32)
    blk = jnp.asarray(same, BF16)
    ip = np.arange(CHUNK)[:, None]
    nstrict = jnp.asarray(-1.0 * (ip > (j % CHUNK)), F32)
    eye = jnp.asarray(ip == (j % CHUNK), F32)
    col = lambda a, off: pl.BlockSpec((None, TILE, cw), lambda b, h, s: (b, s, off // cw + h))
    const = pl.BlockSpec((TILE, TILE), lambda b, h, s: (0, 0))
    packed = pl.BlockSpec((CHUNK, TILE), lambda b, h, s: (0, 0))
    return pl.pallas_call(
        functools.partial(_gdn_body, hb=hb),
        grid=(bsz, nhb, t // TILE),
        in_specs=[col(qkv3, 0), col(qkv3, width), col(qkv3, 2 * width), col(proj3, z_col_off),
                  pl.BlockSpec((None, 8, hb, TILE), lambda b, h, s: (b, 0, h, s)),
                  const, packed, packed, const,
                  pl.BlockSpec((1, d), lambda b, h, s: (0, 0))],
        out_specs=pl.BlockSpec((None, TILE, cw), lambda b, h, s: (b, s, h)),
        out_shape=jax.ShapeDtypeStruct((bsz, t, width), BF16),
        scratch_shapes=[pltpu.VMEM((hb, d, d), F32)],
        compiler_params=_params("parallel", "parallel", "arbitrary"),
        name="gated_delta",
    )(qkv3, qkv3, qkv3, proj3, gates, incl, nstrict, eye, blk, norm_gain.reshape(1, d).astype(F32))


def kernel(x, ln1_gain, w_in, ret_norm_gain, gdn_conv_w, gdn_A_log, gdn_dt_bias, gdn_norm_gain,
           w_out, ln2_gain, w_up, w_down, final_gain):
    bsz, t, dm = x.shape
    depth = w_in.shape[0]
    nh = gdn_A_log.shape[1]
    gw = nh * GDN_HEAD_DIM
    rw = w_out.shape[1] - gw
    main_cols = 4 * rw + 4 * gw
    m = bsz * t
    xf = x.reshape(m, dm)
    for l in range(depth):
        wt = w_in[l].T
        h, gates = _norm_gates(xf, ln1_gain[l], wt, main_cols, gdn_A_log[l], gdn_dt_bias[l], t)
        proj, w_i = _in_proj_first(h, wt, main_cols, BF16)
        proj, w_u, w_o = _matmul(h, w_i, BF16, casts=(w_up[l], w_out[l]), into=proj, name="in_proj")
        proj3 = proj.reshape(bsz, t, main_cols)
        y_ret = _retention(proj3, ret_norm_gain[l], rw)
        qkv = _gdn_prep(proj3, gdn_conv_w[l], 4 * rw, gw)
        y_gdn = _gated_delta(qkv, proj3, 4 * rw + 3 * gw, gates, gdn_norm_gain[l], gw)
        xf, xg, ss = _out_proj(y_ret.reshape(m, rw), y_gdn.reshape(m, gw), w_o, xf, ln2_gain[l])
        act, w_d = _matmul(xg, w_u, BF16, relu2=True, ss=ss, casts=(w_down[l],), name="up_proj")
        if l < depth - 1:
            xf, = _down_proj(act, w_d, xf, (0, m))
    split = (m // DOWN_TM) * 3 // 4 * DOWN_TM
    x_a, = _down_proj(act, w_d, xf, (0, split), tm=DOWN_TM)
    x_b, out = _down_proj(act, w_d, xf, (split, m), side=(x_a, final_gain, m), tm=DOWN_TM, tk=2048)
    out = _rmsnorm(x_b, final_gain, F32, into=(out, split))
    return out.reshape(bsz, t, dm)
```

```python
import functools

import numpy as np
import jax
import jax.numpy as jnp
from jax import lax
from jax.experimental import pallas as pl
from jax.experimental.pallas import tpu as pltpu

F32 = jnp.float32
BF16 = jnp.bfloat16

CHUNK = 64
RET_HEADS = 8
GDN_HEAD_DIM = 128
CONV_WIDTH = 4
ROPE_BASE = 10000.0
EPS = 1e-6

V7X_VMEM_LIMIT_BYTES = 56 * 1024 * 1024
LANES = 128
SUBLANES = 8
BF16_ROWS = 16
TILE = 256
DOWN_TM = 1024
NEG_BIG = -1e30
LOG2E = 1.4426950408889634

_NT = (((1,), (1,)), ((), ()))
_TN = (((0,), (0,)), ((), ()))


def _params(*sem):
    return pltpu.CompilerParams(dimension_semantics=sem, vmem_limit_bytes=V7X_VMEM_LIMIT_BYTES)


def _dot(a, b, dims=None):
    if dims is None:
        return jnp.dot(a, b, preferred_element_type=F32)
    return lax.dot_general(a, b, dims, preferred_element_type=F32)


def _silu(x):
    return x * jax.nn.sigmoid(x)


def _rms(x, gain):
    return x * lax.rsqrt(jnp.mean(x * x, axis=-1, keepdims=True) + EPS) * gain


def _rmsnorm_body(x_ref, g_ref, *rest):
    o_ref = rest[-1]
    o_ref[...] = _rms(x_ref[...].astype(F32), g_ref[...]).astype(o_ref.dtype)


def _rmsnorm(x, gain, out_dtype, tm=512, into=None):
    m, d = x.shape
    tm = min(tm, m)
    in_specs = [pl.BlockSpec((tm, d), lambda i: (i, 0)),
                pl.BlockSpec((1, d), lambda i: (0, 0))]
    args = [x, gain.reshape(1, d).astype(F32)]
    off, out_shape, aliases = 0, (m, d), {}
    if into is not None:
        dst, row0 = into
        assert row0 % tm == 0 and dst.dtype == out_dtype and dst.shape[1] == d
        off, out_shape, aliases = row0 // tm, dst.shape, {2: 0}
        in_specs.append(pl.BlockSpec(memory_space=pl.ANY))
        args.append(dst)
    return pl.pallas_call(
        _rmsnorm_body,
        grid=(m // tm,),
        in_specs=in_specs,
        out_specs=pl.BlockSpec((tm, d), lambda i: (i + off, 0)),
        out_shape=jax.ShapeDtypeStruct(out_shape, out_dtype),
        input_output_aliases=aliases,
        compiler_params=_params("parallel"),
        name="rmsnorm",
    )(*args)


def _cast_specs(casts, n_steps, nj):
    specs = []
    for w in casts:
        rows, cols = w.shape
        assert rows % (BF16_ROWS * n_steps) == 0
        specs.append(pl.BlockSpec((rows // n_steps, cols),
                                  lambda i, j: (jnp.minimum(i * nj + j, n_steps - 1), 0)))
    return specs


def _in_first_body(a_ref, bt_ref, o_ref, wb_ref):
    wb = bt_ref[...].T.astype(wb_ref.dtype)
    wb_ref[...] = wb
    o_ref[...] = _dot(a_ref[...], wb).astype(o_ref.dtype)


def _in_proj_first(a, wt, n, out_dtype, tm=1024, tn=512):
    m, k = a.shape
    tm, tn = min(tm, m), min(tn, n)
    assert m % tm == 0 and n % tn == 0
    return pl.pallas_call(
        _in_first_body,
        grid=(n // tn,),
        in_specs=[pl.BlockSpec((tm, k), lambda j: (0, 0)),
                  pl.BlockSpec((tn, k), lambda j: (j, 0))],
        out_specs=[pl.BlockSpec((tm, tn), lambda j: (0, j)),
                   pl.BlockSpec((k, tn), lambda j: (0, j))],
        out_shape=[jax.ShapeDtypeStruct((m, n), out_dtype), jax.ShapeDtypeStruct((k, n), BF16)],
        compiler_params=_params("parallel"),
        name="in_proj_first",
    )(a, wt)


def _mm_body(*refs, relu2, n_cast, has_ss, has_alias, d_norm):
    a_ref, b_ref = refs[0], refs[1]
    pos = 2
    ss_ref = None
    if has_ss:
        ss_ref, pos = refs[pos], pos + 1
    if has_alias:
        pos += 1
    cast_in = refs[pos:pos + n_cast]
    o_ref = refs[pos + n_cast]
    cast_out = refs[pos + n_cast + 1:]
    acc = _dot(a_ref[...], b_ref[...])
    if has_ss:
        acc = acc * lax.rsqrt(ss_ref[:, 0:1] * (1.0 / d_norm) + EPS)
    if relu2:
        acc = jnp.square(jnp.maximum(acc, 0.0))
    o_ref[...] = acc.astype(o_ref.dtype)
    for src, dst in zip(cast_in, cast_out):
        dst[...] = src[...].astype(dst.dtype)


def _matmul(a, b, out_dtype, *, n=None, relu2=False, ss=None, casts=(), into=None,
            tm=1024, tn=1024, name="matmul"):
    m, k = a.shape
    n = b.shape[1] if n is None else n
    tm, tn = min(tm, m), min(tn, n)
    assert m % tm == 0 and n % tn == 0
    i0 = 0 if into is None else 1
    ni, nj = m // tm - i0, n // tn
    cast_steps = 1 << ((ni * nj).bit_length() - 1)
    in_specs = [pl.BlockSpec((tm, k), lambda i, j: (i + i0, 0)),
                pl.BlockSpec((k, tn), lambda i, j: (0, j))]
    args = [a, b]
    if ss is not None:
        in_specs.append(pl.BlockSpec((tm, ss.shape[1]), lambda i, j: (i + i0, 0)))
        args.append(ss)
    aliases = {}
    if into is not None:
        assert into.shape == (m, n) and into.dtype == out_dtype
        aliases = {len(args): 0}
        in_specs.append(pl.BlockSpec(memory_space=pl.ANY))
        args.append(into)
    return pl.pallas_call(
        functools.partial(_mm_body, relu2=relu2, n_cast=len(casts), has_ss=ss is not None,
                          has_alias=into is not None, d_norm=k),
        grid=(ni, nj),
        in_specs=in_specs + _cast_specs(casts, cast_steps, nj),
        out_specs=[pl.BlockSpec((tm, tn), lambda i, j: (i + i0, j))] + _cast_specs(casts, cast_steps, nj),
        out_shape=[jax.ShapeDtypeStruct((m, n), out_dtype)]
        + [jax.ShapeDtypeStruct(w.shape, BF16) for w in casts],
        input_output_aliases=aliases,
        compiler_params=_params("parallel", "arbitrary"),
        name=name,
    )(*args, *casts)


def _out_proj_body(a1_ref, a2_ref, b1_ref, b2_ref, x_ref, g_ref, o_ref, xg_ref, ss_ref):
    x1 = x_ref[...] + (_dot(a1_ref[...], b1_ref[...]) + _dot(a2_ref[...], b2_ref[...]))
    o_ref[...] = x1
    xg_ref[...] = (x1 * g_ref[...]).astype(xg_ref.dtype)
    part = jnp.broadcast_to(jnp.sum(x1 * x1, axis=-1, keepdims=True), ss_ref.shape)

    @pl.when(pl.program_id(1) == 0)
    def _():
        ss_ref[...] = part

    @pl.when(pl.program_id(1) > 0)
    def _():
        ss_ref[...] += part


def _out_proj(y1, y2, w, x, gain, tm=1024, tn=512):
    m, k1 = y1.shape
    assert y2.shape[1] == k1 and w.shape[0] == 2 * k1
    n = w.shape[1]
    tm, tn = min(tm, m), min(tn, n)
    nj = n // tn
    tile = lambda: pl.BlockSpec((tm, tn), lambda i, j: (i, j))
    return pl.pallas_call(
        _out_proj_body,
        grid=(m // tm, nj),
        in_specs=[pl.BlockSpec((tm, k1), lambda i, j: (i, 0)),
                  pl.BlockSpec((tm, k1), lambda i, j: (i, 0)),
                  pl.BlockSpec((k1, tn), lambda i, j: (0, j)),
                  pl.BlockSpec((k1, tn), lambda i, j: (1, j)),
                  tile(),
                  pl.BlockSpec((1, tn), lambda i, j: (0, j))],
        out_specs=[tile(), tile(), pl.BlockSpec((tm, LANES), lambda i, j: (i, 0))],
        out_shape=[jax.ShapeDtypeStruct((m, n), F32), jax.ShapeDtypeStruct((m, n), BF16),
                   jax.ShapeDtypeStruct((m, LANES), F32)],
        compiler_params=_params("parallel", "arbitrary"),
        name="out_proj",
    )(y1, y2, w, w, x, gain.reshape(1, n).astype(F32))


def _down_body(a_ref, b_ref, x_ref, *rest):
    o_ref = rest[-1] if len(rest) == 1 else rest[2]
    @pl.when(pl.program_id(2) == 0)
    def _():
        o_ref[...] = x_ref[...]

    o_ref[...] += _dot(a_ref[...], b_ref[...])
    if len(rest) > 1:
        src_ref, g_ref, _, dst_ref = rest
        dst_ref[...] = _rms(src_ref[...], g_ref[...])


def _down_proj(a, b, x, rows, *, side=None, tm=1024, tn=1024, tk=4096):
    m, k = a.shape
    n = b.shape[1]
    tm, tn, tk = min(tm, m), min(tn, n), min(tk, k)
    assert rows[0] % tm == 0 and rows[1] % tm == 0
    r0, ni, nj, nk = rows[0] // tm, (rows[1] - rows[0]) // tm, n // tn, k // tk
    in_specs = [pl.BlockSpec((tm, tk), lambda i, j, kk: (i + r0, kk)),
                pl.BlockSpec((tk, tn), lambda i, j, kk: (kk, j)),
                pl.BlockSpec((tm, tn), lambda i, j, kk: (i + r0, j))]
    out_specs = [pl.BlockSpec((tm, tn), lambda i, j, kk: (i, j))]
    out_shape = [jax.ShapeDtypeStruct((ni * tm, n), F32)]
    args = [a, b, x]
    if side is not None:
        src, gain, m_total = side
        steps = ni * nj * nk
        rs = src.shape[0] // steps
        assert src.shape[0] % steps == 0 and rs % SUBLANES == 0 and src.shape[1] == n
        lin = lambda i, j, kk: ((i * nj + j) * nk + kk, 0)
        in_specs += [pl.BlockSpec((rs, n), lin), pl.BlockSpec((1, n), lambda i, j, kk: (0, 0))]
        out_specs.append(pl.BlockSpec((rs, n), lin))
        out_shape.append(jax.ShapeDtypeStruct((m_total, n), F32))
        args += [src, gain.reshape(1, n).astype(F32)]
    return pl.pallas_call(
        _down_body,
        grid=(ni, nj, nk),
        in_specs=in_specs,
        out_specs=out_specs,
        out_shape=out_shape,
        compiler_params=_params("parallel", "parallel", "arbitrary"),
        name="down_proj",
    )(*args)


def _split3(x):
    hi = x.astype(BF16)
    r1 = x - hi.astype(F32)
    mid = r1.astype(BF16)
    lo = (r1 - mid.astype(F32)).astype(BF16)
    return hi, mid, lo


def _norm_gates_body(x_ref, gain_ref, tail_ref, alog_ref, dtb_ref, ucum_ref, ublk_ref, h_ref, o_ref, *, nh):
    h = _rms(x_ref[...], gain_ref[...]).astype(h_ref.dtype)
    h_ref[...] = h
    logits = _dot(tail_ref[0:2 * nh, :].astype(h.dtype), h, _NT)
    beta = jax.nn.sigmoid(logits[:nh])
    g = -jnp.exp(alog_ref[...]) * jax.nn.softplus(logits[nh:] + dtb_ref[...])
    parts = _split3(g)
    ucum, ublk = ucum_ref[...], ublk_ref[...]
    gc = (_dot(parts[0], ucum) + _dot(parts[1], ucum)) + _dot(parts[2], ucum)
    gl = (_dot(parts[0], ublk) + _dot(parts[1], ublk)) + _dot(parts[2], ublk)
    o_ref[0] = beta
    o_ref[1] = gc
    o_ref[2] = jnp.exp(gc)
    o_ref[3] = jnp.exp(gl - gc)
    o_ref[4] = jnp.exp(gl)
    o_ref[5] = gc * LOG2E
    zero = jnp.zeros_like(beta)
    o_ref[6] = zero
    o_ref[7] = zero


def _norm_gates(x, gain, wt, n, a_log, dt_bias, t, tm=512):
    m, d = x.shape
    nh = a_log.shape[0]
    tm = min(tm, t)
    assert t % tm == 0 and tm % CHUNK == 0 and n % LANES == 0 and n + 2 * nh == wt.shape[0]
    nt = t // tm
    pos = np.arange(tm)
    same = (pos[:, None] // CHUNK) == (pos[None, :] // CHUNK)
    ucum = jnp.asarray(same & (pos[:, None] <= pos[None, :]), BF16)
    ublk = jnp.asarray(same, BF16)
    const = lambda shape: pl.BlockSpec(shape, lambda i: (0, 0))
    return pl.pallas_call(
        functools.partial(_norm_gates_body, nh=nh),
        grid=(m // tm,),
        in_specs=[pl.BlockSpec((tm, d), lambda i: (i, 0)),
                  const((1, d)),
                  pl.BlockSpec((LANES, d), lambda i: (n // LANES, 0)),
                  const((nh, 1)), const((nh, 1)), const((tm, tm)), const((tm, tm))],
        out_specs=[pl.BlockSpec((tm, d), lambda i: (i, 0)),
                   pl.BlockSpec((None, 8, nh, tm), lambda i: (i // nt, 0, 0, i % nt))],
        out_shape=[jax.ShapeDtypeStruct((m, d), BF16), jax.ShapeDtypeStruct((m // t, 8, nh, t), F32)],
        compiler_params=_params("parallel"),
        name="norm_gates",
    )(x, gain.reshape(1, d).astype(F32), wt, a_log.reshape(nh, 1).astype(F32),
      dt_bias.reshape(nh, 1).astype(F32), ucum, ublk)


def _retention_tables(t, hd):
    half = hd // 2
    inv = ROPE_BASE ** (-np.arange(half, dtype=np.float64) * (2.0 / hd))
    ang = np.arange(t, dtype=np.float64)[:, None] * inv[None, :]
    lg = np.log1p(-np.exp2(-5.0 - np.arange(RET_HEADS, dtype=np.float64)))[:, None, None]
    i = np.arange(TILE)[:, None]
    j = np.arange(TILE)[None, :]
    dmask = np.exp(lg * np.abs(i - j)) * ((j // CHUNK) <= (i // CHUNK))
    xi = np.broadcast_to(np.exp(lg * (i + 1.0)), (RET_HEADS, TILE, hd))
    zeta = np.broadcast_to(np.exp(lg * (TILE - 1.0 - i)), (RET_HEADS, TILE, hd))
    gtile = np.broadcast_to(np.exp(lg * TILE), (RET_HEADS, 1, hd))
    f = lambda a: jnp.asarray(np.ascontiguousarray(a), F32)
    return f(np.cos(ang)), f(np.sin(ang)), f(dmask), f(xi), f(zeta), f(gtile)


def _ret_body(q_ref, k_ref, v_ref, g_ref, cos_ref, sin_ref, dm_ref, xi_ref, zeta_ref, gt_ref,
              gain_ref, o_ref, state_ref, *, hd, hb):
    @pl.when(pl.program_id(2) == 0)
    def _():
        state_ref[...] = jnp.zeros_like(state_ref)

    half = hd // 2
    cos, sin = cos_ref[...], sin_ref[...]

    def rope(x, c, s):
        x1, x2 = x[:, :half], x[:, half:]
        return jnp.concatenate([x1 * c - x2 * s, x1 * s + x2 * c], axis=-1)

    cos_k, sin_k = cos * (hd ** -0.5), sin * (hd ** -0.5)
    heads = range(hb)
    sls = [slice(l * hd, (l + 1) * hd) for l in heads]
    qb = [rope(q_ref[:, sls[l]].astype(F32), cos, sin).astype(BF16) for l in heads]
    k = [rope(k_ref[:, sls[l]].astype(F32), cos_k, sin_k) for l in heads]
    v = [v_ref[:, sls[l]].astype(BF16) for l in heads]
    scores = [(_dot(qb[l], k[l].astype(BF16), _NT) * dm_ref[l]).astype(BF16) for l in heads]
    state = [state_ref[l] for l in heads]
    o = [_dot(scores[l], v[l]) + xi_ref[l] * _dot(qb[l], state[l].astype(BF16)) for l in heads]
    for l in heads:
        k_dec = (k[l] * zeta_ref[l]).astype(BF16)
        state_ref[l] = state[l] * gt_ref[l] + _dot(k_dec, v[l], _TN)
    for l in heads:
        mu = jnp.mean(o[l], axis=-1, keepdims=True)
        oc = o[l] - mu
        var = jnp.mean(oc * oc, axis=-1, keepdims=True)
        y = oc * lax.rsqrt(var + EPS) * gain_ref[:, sls[l]]
        o_ref[:, sls[l]] = (y * _silu(g_ref[:, sls[l]].astype(F32))).astype(o_ref.dtype)


def _retention(proj3, ret_gain, ret_width, hb=8):
    bsz, t, _ = proj3.shape
    hd = ret_width // RET_HEADS
    nhb = RET_HEADS // hb
    cw = hb * hd
    cos, sin, dmask, xi, zeta, gtile = _retention_tables(t, hd)
    col = lambda sec: pl.BlockSpec((None, TILE, cw), lambda b, h, i: (b, i, sec * nhb + h))
    head = lambda r, c: pl.BlockSpec((hb, r, c), lambda b, h, i: (h, 0, 0))
    return pl.pallas_call(
        functools.partial(_ret_body, hd=hd, hb=hb),
        grid=(bsz, nhb, t // TILE),
        in_specs=[col(0), col(1), col(2), col(3),
                  pl.BlockSpec((TILE, hd // 2), lambda b, h, i: (i, 0)),
                  pl.BlockSpec((TILE, hd // 2), lambda b, h, i: (i, 0)),
                  head(TILE, TILE), head(TILE, hd), head(TILE, hd), head(1, hd),
                  pl.BlockSpec((1, cw), lambda b, h, i: (0, h))],
        out_specs=pl.BlockSpec((None, TILE, cw), lambda b, h, i: (b, i, h)),
        out_shape=jax.ShapeDtypeStruct((bsz, t, ret_width), BF16),
        scratch_shapes=[pltpu.VMEM((hb, hd, hd), F32)],
        compiler_params=_params("parallel", "parallel", "arbitrary"),
        name="retention",
    )(proj3, proj3, proj3, proj3, cos, sin, dmask, xi, zeta, gtile,
      ret_gain.reshape(1, ret_width).astype(F32))


def _prep_body(x_ref, w_ref, o_ref, xbuf_ref, *, tt, n_qk_blocks, n_q_blocks):
    halo = SUBLANES

    @pl.when(pl.program_id(2) == 0)
    def _():
        xbuf_ref[0:halo, :] = jnp.zeros((halo, xbuf_ref.shape[1]), F32)

    x = x_ref[...].astype(F32)
    xbuf_ref[halo:halo + tt, :] = x
    w = w_ref[...]
    acc = w[CONV_WIDTH - 1:CONV_WIDTH, :] * x
    for j in range(CONV_WIDTH - 1):
        acc = acc + w[j:j + 1, :] * xbuf_ref[pl.ds(halo - (CONV_WIDTH - 1) + j, tt), :]
    xbuf_ref[0:halo, :] = x[tt - halo:tt, :]
    y = _silu(acc)
    c = pl.program_id(1)

    @pl.when(c < n_qk_blocks)
    def _():
        scale = jnp.where(c < n_q_blocks, GDN_HEAD_DIM ** -0.5, 1.0).astype(F32)
        for s in range(y.shape[1] // GDN_HEAD_DIM):
            sl = slice(s * GDN_HEAD_DIM, (s + 1) * GDN_HEAD_DIM)
            yh = y[:, sl]
            ss = jnp.sum(yh * yh, axis=-1, keepdims=True)
            o_ref[:, sl] = (yh * (lax.rsqrt(ss + EPS) * scale)).astype(o_ref.dtype)

    @pl.when(c >= n_qk_blocks)
    def _():
        o_ref[...] = y.astype(o_ref.dtype)


def _gdn_prep(proj3, conv_w, col_off, width, tt=1024, cb=512):
    bsz, t, _ = proj3.shape
    tt = min(tt, t)
    ncb = 3 * width // cb
    return pl.pallas_call(
        functools.partial(_prep_body, tt=tt, n_qk_blocks=2 * width // cb, n_q_blocks=width // cb),
        grid=(bsz, ncb, t // tt),
        in_specs=[pl.BlockSpec((None, tt, cb), lambda b, c, i: (b, i, col_off // cb + c)),
                  pl.BlockSpec((CONV_WIDTH, cb), lambda b, c, i: (0, c))],
        out_specs=pl.BlockSpec((None, tt, cb), lambda b, c, i: (b, i, c)),
        out_shape=jax.ShapeDtypeStruct((bsz, t, 3 * width), BF16),
        scratch_shapes=[pltpu.VMEM((tt + 8, cb), F32)],
        compiler_params=_params("parallel", "parallel", "arbitrary"),
        name="gdn_prep",
    )(proj3, conv_w.astype(F32))


def _gdn_body(q_ref, k_ref, v_ref, z_ref, gate_ref, incl_ref, nstrict_ref, eye_ref, blk_ref, gain_ref,
              o_ref, state_ref, *, hb):
    d = GDN_HEAD_DIM
    nck = TILE // CHUNK

    @pl.when(pl.program_id(2) == 0)
    def _():
        state_ref[...] = jnp.zeros_like(state_ref)

    rows = jnp.concatenate([gate_ref[c] for c in range(6)]
                           + [jnp.zeros((LANES - 6 * hb, TILE), F32)], axis=0)
    cols = rows.T
    incl = incl_ref[...] > 0.0
    nstrict = nstrict_ref[...]
    eye = eye_ref[...]
    blk = blk_ref[...]
    gain = gain_ref[...]

    def block_diag(packed):
        return jnp.concatenate([packed] * nck, axis=0) * blk

    heads = range(hb)
    sls = [slice(l * d, (l + 1) * d) for l in heads]
    col = lambda c, l: cols[:, c * hb + l:c * hb + l + 1]
    kbs = [k_ref[:, sls[l]] for l in heads]
    ks = [kb.astype(F32) for kb in kbs]
    decay = [jnp.exp2(jnp.where(incl, col(5, l) - gate_ref[5, l:l + 1, :], NEG_BIG)) for l in heads]
    k_beta = [ks[l] * col(0, l) for l in heads]
    kkd = [_dot(k_beta[l].astype(BF16), kbs[l], _NT) * decay[l] for l in heads]
    p = [sum(kkd[l][c * CHUNK:(c + 1) * CHUNK] for c in range(nck)) * nstrict for l in heads]
    acc = [eye + p[l] for l in heads]
    pw = [p[l].astype(BF16) for l in heads]
    pw = [_dot(pw[l], block_diag(pw[l])).astype(BF16) for l in heads]
    for _ in range(4):
        res = [_dot(jnp.concatenate([acc[l].astype(BF16), pw[l]], axis=0), block_diag(pw[l])) for l in heads]
        acc = [acc[l] + res[l][:CHUNK] for l in heads]
        pw = [res[l][CHUNK:].astype(BF16) for l in heads]
    acc = [acc[l] + _dot(acc[l].astype(BF16), block_diag(pw[l])) for l in heads]
    rhs = [jnp.concatenate([v_ref[:, sls[l]].astype(F32) * col(0, l), k_beta[l] * col(2, l)],
                           axis=-1).astype(BF16) for l in heads]
    sol = [_dot(block_diag(acc[l].astype(BF16)), rhs[l]).astype(BF16) for l in heads]
    qs = [q_ref[:, sls[l]].astype(F32) for l in heads]
    attn = [(_dot(qs[l].astype(BF16), kbs[l], _NT) * decay[l]).astype(BF16) for l in heads]
    au_aw = [_dot(attn[l], sol[l]) for l in heads]
    q_eff = [(qs[l] * col(2, l) - au_aw[l][:, d:]).astype(BF16) for l in heads]
    k_dec = [(ks[l] * col(3, l)).astype(BF16) for l in heads]

    state = [state_ref[l] for l in heads]
    outs = [[] for _ in heads]
    for c in range(nck):
        r = slice(c * CHUNK, (c + 1) * CHUNK)
        ktuw = [_dot(k_dec[l][r], sol[l][r], _TN) for l in heads]
        res = [_dot(jnp.concatenate([q_eff[l][r], ktuw[l][:, d:].astype(BF16)], axis=0),
                    state[l].astype(BF16)) for l in heads]
        for l in heads:
            outs[l].append(res[l][:CHUNK] + au_aw[l][r, :d])
            cd = cols[c * CHUNK:c * CHUNK + 1, 4 * hb + l:4 * hb + l + 1]
            state[l] = state[l] * cd + ktuw[l][:, :d] - res[l][CHUNK:]
    for l in heads:
        state_ref[l] = state[l]
        o = jnp.concatenate(outs[l], axis=0)
        o = o * lax.rsqrt(jnp.mean(o * o, axis=-1, keepdims=True) + EPS) * gain
        o_ref[:, sls[l]] = (o * _silu(z_ref[:, sls[l]].astype(F32))).astype(o_ref.dtype)


def _gated_delta(qkv3, proj3, z_col_off, gates, norm_gain, width, hb=8):
    bsz, t, _ = qkv3.shape
    d = GDN_HEAD_DIM
    nh = width // d
    hb = min(hb, nh)
    nhb = nh // hb
    cw = hb * d
    i = np.arange(TILE)[:, None]
    j = np.arange(TILE)[None, :]
    same = (i // CHUNK) == (j // CHUNK)
    incl = jnp.asarray(same & (i >= j), F32)
    blk = jnp.asarray(same, BF16)
    ip = np.arange(CHUNK)[:, None]
    nstrict = jnp.asarray(-1.0 * (ip > (j % CHUNK)), F32)
    eye = jnp.asarray(ip == (j % CHUNK), F32)
    col = lambda a, off: pl.BlockSpec((None, TILE, cw), lambda b, h, s: (b, s, off // cw + h))
    const = pl.BlockSpec((TILE, TILE), lambda b, h, s: (0, 0))
    packed = pl.BlockSpec((CHUNK, TILE), lambda b, h, s: (0, 0))
    return pl.pallas_call(
        functools.partial(_gdn_body, hb=hb),
        grid=(bsz, nhb, t // TILE),
        in_specs=[col(qkv3, 0), col(qkv3, width), col(qkv3, 2 * width), col(proj3, z_col_off),
                  pl.BlockSpec((None, 8, hb, TILE), lambda b, h, s: (b, 0, h, s)),
                  const, packed, packed, const,
                  pl.BlockSpec((1, d), lambda b, h, s: (0, 0))],
        out_specs=pl.BlockSpec((None, TILE, cw), lambda b, h, s: (b, s, h)),
        out_shape=jax.ShapeDtypeStruct((bsz, t, width), BF16),
        scratch_shapes=[pltpu.VMEM((hb, d, d), F32)],
        compiler_params=_params("parallel", "parallel", "arbitrary"),
        name="gated_delta",
    )(qkv3, qkv3, qkv3, proj3, gates, incl, nstrict, eye, blk, norm_gain.reshape(1, d).astype(F32))


def kernel(x, ln1_gain, w_in, ret_norm_gain, gdn_conv_w, gdn_A_log, gdn_dt_bias, gdn_norm_gain,
           w_out, ln2_gain, w_up, w_down, final_gain):
    bsz, t, dm = x.shape
    depth = w_in.shape[0]
    nh = gdn_A_log.shape[1]
    gw = nh * GDN_HEAD_DIM
    rw = w_out.shape[1] - gw
    main_cols = 4 * rw + 4 * gw
    m = bsz * t
    xf = x.reshape(m, dm)
    for l in range(depth):
        wt = w_in[l].T
        h, gates = _norm_gates(xf, ln1_gain[l], wt, main_cols, gdn_A_log[l], gdn_dt_bias[l], t)
        proj, w_i = _in_proj_first(h, wt, main_cols, BF16)
        proj, w_u, w_o = _matmul(h, w_i, BF16, casts=(w_up[l], w_out[l]), into=proj, name="in_proj")
        proj3 = proj.reshape(bsz, t, main_cols)
        y_ret = _retention(proj3, ret_norm_gain[l], rw)
        qkv = _gdn_prep(proj3, gdn_conv_w[l], 4 * rw, gw)
        y_gdn = _gated_delta(qkv, proj3, 4 * rw + 3 * gw, gates, gdn_norm_gain[l], gw)
        xf, xg, ss = _out_proj(y_ret.reshape(m, rw), y_gdn.reshape(m, gw), w_o, xf, ln2_gain[l])
        act, w_d = _matmul(xg, w_u, BF16, relu2=True, ss=ss, casts=(w_down[l],), name="up_proj")
        if l < depth - 1:
            xf, = _down_proj(act, w_d, xf, (0, m))
    split = (m // DOWN_TM) * 7 // 8 * DOWN_TM
    x_a, = _down_proj(act, w_d, xf, (0, split), tm=DOWN_TM)
    x_b, out = _down_proj(act, w_d, xf, (split, m), side=(x_a, final_gain, m), tm=DOWN_TM, tk=2048)
    out = _rmsnorm(x_b, final_gain, F32, into=(out, split))
    return out.reshape(bsz, t, dm)
```

```python
import functools

import numpy as np
import jax
import jax.numpy as jnp
from jax import lax
from jax.experimental import pallas as pl
from jax.experimental.pallas import tpu as pltpu

F32 = jnp.float32
BF16 = jnp.bfloat16

CHUNK = 64
RET_HEADS = 8
GDN_HEAD_DIM = 128
CONV_WIDTH = 4
ROPE_BASE = 10000.0
EPS = 1e-6

V7X_VMEM_LIMIT_BYTES = 56 * 1024 * 1024
LANES = 128
SUBLANES = 8
BF16_ROWS = 16
TILE = 256
DOWN_TM = 1024
NEG_BIG = -1e30
LOG2E = 1.4426950408889634

_NT = (((1,), (1,)), ((), ()))
_TN = (((0,), (0,)), ((), ()))


def _params(*sem):
    return pltpu.CompilerParams(dimension_semantics=sem, vmem_limit_bytes=V7X_VMEM_LIMIT_BYTES)


def _dot(a, b, dims=None):
    if dims is None:
        return jnp.dot(a, b, preferred_element_type=F32)
    return lax.dot_general(a, b, dims, preferred_element_type=F32)


def _silu(x):
    return x * jax.nn.sigmoid(x)


def _rms(x, gain):
    return x * lax.rsqrt(jnp.mean(x * x, axis=-1, keepdims=True) + EPS) * gain


def _rmsnorm_body(x_ref, g_ref, *rest):
    o_ref = rest[-1]
    o_ref[...] = _rms(x_ref[...].astype(F32), g_ref[...]).astype(o_ref.dtype)


def _rmsnorm(x, gain, out_dtype, tm=512, into=None):
    m, d = x.shape
    tm = min(tm, m)
    in_specs = [pl.BlockSpec((tm, d), lambda i: (i, 0)),
                pl.BlockSpec((1, d), lambda i: (0, 0))]
    args = [x, gain.reshape(1, d).astype(F32)]
    off, out_shape, aliases = 0, (m, d), {}
    if into is not None:
        dst, row0 = into
        assert row0 % tm == 0 and dst.dtype == out_dtype and dst.shape[1] == d
        off, out_shape, aliases = row0 // tm, dst.shape, {2: 0}
        in_specs.append(pl.BlockSpec(memory_space=pl.ANY))
        args.append(dst)
    return pl.pallas_call(
        _rmsnorm_body,
        grid=(m // tm,),
        in_specs=in_specs,
        out_specs=pl.BlockSpec((tm, d), lambda i: (i + off, 0)),
        out_shape=jax.ShapeDtypeStruct(out_shape, out_dtype),
        input_output_aliases=aliases,
        compiler_params=_params("parallel"),
        name="rmsnorm",
    )(*args)


def _cast_specs(casts, n_steps, nj):
    specs = []
    for w in casts:
        rows, cols = w.shape
        assert rows % (BF16_ROWS * n_steps) == 0
        specs.append(pl.BlockSpec((rows // n_steps, cols),
                                  lambda i, j: (jnp.minimum(i * nj + j, n_steps - 1), 0)))
    return specs


def _in_first_body(a_ref, bt_ref, o_ref, wb_ref):
    wb = bt_ref[...].T.astype(wb_ref.dtype)
    wb_ref[...] = wb
    o_ref[...] = _dot(a_ref[...], wb).astype(o_ref.dtype)


def _in_proj_first(a, wt, n, out_dtype, tm=1024, tn=512):
    m, k = a.shape
    tm, tn = min(tm, m), min(tn, n)
    assert m % tm == 0 and n % tn == 0
    return pl.pallas_call(
        _in_first_body,
        grid=(n // tn,),
        in_specs=[pl.BlockSpec((tm, k), lambda j: (0, 0)),
                  pl.BlockSpec((tn, k), lambda j: (j, 0))],
        out_specs=[pl.BlockSpec((tm, tn), lambda j: (0, j)),
                   pl.BlockSpec((k, tn), lambda j: (0, j))],
        out_shape=[jax.ShapeDtypeStruct((m, n), out_dtype), jax.ShapeDtypeStruct((k, n), BF16)],
        compiler_params=_params("parallel"),
        name="in_proj_first",
    )(a, wt)


def _mm_body(*refs, relu2, n_cast, has_ss, has_alias, d_norm):
    a_ref, b_ref = refs[0], refs[1]
    pos = 2
    ss_ref = None
    if has_ss:
        ss_ref, pos = refs[pos], pos + 1
    if has_alias:
        pos += 1
    cast_in = refs[pos:pos + n_cast]
    o_ref = refs[pos + n_cast]
    cast_out = refs[pos + n_cast + 1:]
    acc = _dot(a_ref[...], b_ref[...])
    if has_ss:
        acc = acc * lax.rsqrt(ss_ref[:, 0:1] * (1.0 / d_norm) + EPS)
    if relu2:
        acc = jnp.square(jnp.maximum(acc, 0.0))
    o_ref[...] = acc.astype(o_ref.dtype)
    for src, dst in zip(cast_in, cast_out):
        dst[...] = src[...].astype(dst.dtype)


def _matmul(a, b, out_dtype, *, n=None, relu2=False, ss=None, casts=(), into=None,
            tm=1024, tn=1024, name="matmul"):
    m, k = a.shape
    n = b.shape[1] if n is None else n
    tm, tn = min(tm, m), min(tn, n)
    assert m % tm == 0 and n % tn == 0
    i0 = 0 if into is None else 1
    ni, nj = m // tm - i0, n // tn
    cast_steps = 1 << ((ni * nj).bit_length() - 1)
    in_specs = [pl.BlockSpec((tm, k), lambda i, j: (i + i0, 0)),
                pl.BlockSpec((k, tn), lambda i, j: (0, j))]
    args = [a, b]
    if ss is not None:
        in_specs.append(pl.BlockSpec((tm, ss.shape[1]), lambda i, j: (i + i0, 0)))
        args.append(ss)
    aliases = {}
    if into is not None:
        assert into.shape == (m, n) and into.dtype == out_dtype
        aliases = {len(args): 0}
        in_specs.append(pl.BlockSpec(memory_space=pl.ANY))
        args.append(into)
    return pl.pallas_call(
        functools.partial(_mm_body, relu2=relu2, n_cast=len(casts), has_ss=ss is not None,
                          has_alias=into is not None, d_norm=k),
        grid=(ni, nj),
        in_specs=in_specs + _cast_specs(casts, cast_steps, nj),
        out_specs=[pl.BlockSpec((tm, tn), lambda i, j: (i + i0, j))] + _cast_specs(casts, cast_steps, nj),
        out_shape=[jax.ShapeDtypeStruct((m, n), out_dtype)]
        + [jax.ShapeDtypeStruct(w.shape, BF16) for w in casts],
        input_output_aliases=aliases,
        compiler_params=_params("parallel", "arbitrary"),
        name=name,
    )(*args, *casts)


def _out_proj_body(a1_ref, a2_ref, b1_ref, b2_ref, x_ref, g_ref, o_ref, xg_ref, ss_ref):
    x1 = x_ref[...] + (_dot(a1_ref[...], b1_ref[...]) + _dot(a2_ref[...], b2_ref[...]))
    o_ref[...] = x1
    xg_ref[...] = (x1 * g_ref[...]).astype(xg_ref.dtype)
    part = jnp.broadcast_to(jnp.sum(x1 * x1, axis=-1, keepdims=True), ss_ref.shape)

    @pl.when(pl.program_id(1) == 0)
    def _():
        ss_ref[...] = part

    @pl.when(pl.program_id(1) > 0)
    def _():
        ss_ref[...] += part


def _out_proj(y1, y2, w, x, gain, tm=1024, tn=512):
    m, k1 = y1.shape
    assert y2.shape[1] == k1 and w.shape[0] == 2 * k1
    n = w.shape[1]
    tm, tn = min(tm, m), min(tn, n)
    nj = n // tn
    tile = lambda: pl.BlockSpec((tm, tn), lambda i, j: (i, j))
    return pl.pallas_call(
        _out_proj_body,
        grid=(m // tm, nj),
        in_specs=[pl.BlockSpec((tm, k1), lambda i, j: (i, 0)),
                  pl.BlockSpec((tm, k1), lambda i, j: (i, 0)),
                  pl.BlockSpec((k1, tn), lambda i, j: (0, j)),
                  pl.BlockSpec((k1, tn), lambda i, j: (1, j)),
                  tile(),
                  pl.BlockSpec((1, tn), lambda i, j: (0, j))],
        out_specs=[tile(), tile(), pl.BlockSpec((tm, LANES), lambda i, j: (i, 0))],
        out_shape=[jax.ShapeDtypeStruct((m, n), F32), jax.ShapeDtypeStruct((m, n), BF16),
                   jax.ShapeDtypeStruct((m, LANES), F32)],
        compiler_params=_params("parallel", "arbitrary"),
        name="out_proj",
    )(y1, y2, w, w, x, gain.reshape(1, n).astype(F32))


def _down_body(a_ref, b_ref, x_ref, *rest):
    o_ref = rest[-1] if len(rest) == 1 else rest[2]
    @pl.when(pl.program_id(2) == 0)
    def _():
        o_ref[...] = x_ref[...]

    o_ref[...] += _dot(a_ref[...], b_ref[...])
    if len(rest) > 1:
        src_ref, g_ref, _, dst_ref = rest
        dst_ref[...] = _rms(src_ref[...], g_ref[...])


def _down_proj(a, b, x, rows, *, side=None, tm=1024, tn=1024, tk=4096):
    m, k = a.shape
    n = b.shape[1]
    tm, tn, tk = min(tm, m), min(tn, n), min(tk, k)
    assert rows[0] % tm == 0 and rows[1] % tm == 0
    r0, ni, nj, nk = rows[0] // tm, (rows[1] - rows[0]) // tm, n // tn, k // tk
    in_specs = [pl.BlockSpec((tm, tk), lambda i, j, kk: (i + r0, kk)),
                pl.BlockSpec((tk, tn), lambda i, j, kk: (kk, j)),
                pl.BlockSpec((tm, tn), lambda i, j, kk: (i + r0, j))]
    out_specs = [pl.BlockSpec((tm, tn), lambda i, j, kk: (i, j))]
    out_shape = [jax.ShapeDtypeStruct((ni * tm, n), F32)]
    args = [a, b, x]
    if side is not None:
        src, gain, m_total = side
        steps = ni * nj * nk
        rs = src.shape[0] // steps
        assert src.shape[0] % steps == 0 and rs % SUBLANES == 0 and src.shape[1] == n
        lin = lambda i, j, kk: ((i * nj + j) * nk + kk, 0)
        in_specs += [pl.BlockSpec((rs, n), lin), pl.BlockSpec((1, n), lambda i, j, kk: (0, 0))]
        out_specs.append(pl.BlockSpec((rs, n), lin))
        out_shape.append(jax.ShapeDtypeStruct((m_total, n), F32))
        args += [src, gain.reshape(1, n).astype(F32)]
    return pl.pallas_call(
        _down_body,
        grid=(ni, nj, nk),
        in_specs=in_specs,
        out_specs=out_specs,
        out_shape=out_shape,
        compiler_params=_params("parallel", "parallel", "arbitrary"),
        name="down_proj",
    )(*args)


def _split3(x):
    hi = x.astype(BF16)
    r1 = x - hi.astype(F32)
    mid = r1.astype(BF16)
    lo = (r1 - mid.astype(F32)).astype(BF16)
    return hi, mid, lo


def _norm_gates_body(x_ref, gain_ref, tail_ref, alog_ref, dtb_ref, ucum_ref, ublk_ref, h_ref, o_ref, *, nh):
    h = _rms(x_ref[...], gain_ref[...]).astype(h_ref.dtype)
    h_ref[...] = h
    logits = _dot(tail_ref[0:2 * nh, :].astype(h.dtype), h, _NT)
    beta = jax.nn.sigmoid(logits[:nh])
    g = -jnp.exp(alog_ref[...]) * jax.nn.softplus(logits[nh:] + dtb_ref[...])
    parts = _split3(g)
    ucum, ublk = ucum_ref[...], ublk_ref[...]
    gc = (_dot(parts[0], ucum) + _dot(parts[1], ucum)) + _dot(parts[2], ucum)
    gl = (_dot(parts[0], ublk) + _dot(parts[1], ublk)) + _dot(parts[2], ublk)
    o_ref[0] = beta
    o_ref[1] = gc
    o_ref[2] = jnp.exp(gc)
    o_ref[3] = jnp.exp(gl - gc)
    o_ref[4] = jnp.exp(gl)
    o_ref[5] = gc * LOG2E
    zero = jnp.zeros_like(beta)
    o_ref[6] = zero
    o_ref[7] = zero


def _norm_gates(x, gain, wt, n, a_log, dt_bias, t, tm=512):
    m, d = x.shape
    nh = a_log.shape[0]
    tm = min(tm, t)
    assert t % tm == 0 and tm % CHUNK == 0 and n % LANES == 0 and n + 2 * nh == wt.shape[0]
    nt = t // tm
    pos = np.arange(tm)
    same = (pos[:, None] // CHUNK) == (pos[None, :] // CHUNK)
    ucum = jnp.asarray(same & (pos[:, None] <= pos[None, :]), BF16)
    ublk = jnp.asarray(same, BF16)
    const = lambda shape: pl.BlockSpec(shape, lambda i: (0, 0))
    return pl.pallas_call(
        functools.partial(_norm_gates_body, nh=nh),
        grid=(m // tm,),
        in_specs=[pl.BlockSpec((tm, d), lambda i: (i, 0)),
                  const((1, d)),
                  pl.BlockSpec((LANES, d), lambda i: (n // LANES, 0)),
                  const((nh, 1)), const((nh, 1)), const((tm, tm)), const((tm, tm))],
        out_specs=[pl.BlockSpec((tm, d), lambda i: (i, 0)),
                   pl.BlockSpec((None, 8, nh, tm), lambda i: (i // nt, 0, 0, i % nt))],
        out_shape=[jax.ShapeDtypeStruct((m, d), BF16), jax.ShapeDtypeStruct((m // t, 8, nh, t), F32)],
        compiler_params=_params("parallel"),
        name="norm_gates",
    )(x, gain.reshape(1, d).astype(F32), wt, a_log.reshape(nh, 1).astype(F32),
      dt_bias.reshape(nh, 1).astype(F32), ucum, ublk)


def _retention_tables(t, hd):
    half = hd // 2
    inv = ROPE_BASE ** (-np.arange(half, dtype=np.float64) * (2.0 / hd))
    ang = np.arange(t, dtype=np.float64)[:, None] * inv[None, :]
    lg = np.log1p(-np.exp2(-5.0 - np.arange(RET_HEADS, dtype=np.float64)))[:, None, None]
    i = np.arange(TILE)[:, None]
    j = np.arange(TILE)[None, :]
    dmask = np.exp(lg * np.abs(i - j)) * ((j // CHUNK) <= (i // CHUNK))
    xi = np.broadcast_to(np.exp(lg * (i + 1.0)), (RET_HEADS, TILE, hd))
    zeta = np.broadcast_to(np.exp(lg * (TILE - 1.0 - i)), (RET_HEADS, TILE, hd))
    gtile = np.broadcast_to(np.exp(lg * TILE), (RET_HEADS, 1, hd))
    f = lambda a: jnp.asarray(np.ascontiguousarray(a), F32)
    return f(np.cos(ang)), f(np.sin(ang)), f(dmask), f(xi), f(zeta), f(gtile)


def _ret_body(q_ref, k_ref, v_ref, g_ref, cos_ref, sin_ref, dm_ref, xi_ref, zeta_ref, gt_ref,
              gain_ref, o_ref, state_ref, *, hd, hb):
    @pl.when(pl.program_id(2) == 0)
    def _():
        state_ref[...] = jnp.zeros_like(state_ref)

    half = hd // 2
    cos, sin = cos_ref[...], sin_ref[...]

    def rope(x, c, s):
        x1, x2 = x[:, :half], x[:, half:]
        return jnp.concatenate([x1 * c - x2 * s, x1 * s + x2 * c], axis=-1)

    cos_k, sin_k = cos * (hd ** -0.5), sin * (hd ** -0.5)
    heads = range(hb)
    sls = [slice(l * hd, (l + 1) * hd) for l in heads]
    qb = [rope(q_ref[:, sls[l]].astype(F32), cos, sin).astype(BF16) for l in heads]
    k = [rope(k_ref[:, sls[l]].astype(F32), cos_k, sin_k) for l in heads]
    v = [v_ref[:, sls[l]].astype(BF16) for l in heads]
    scores = [(_dot(qb[l], k[l].astype(BF16), _NT) * dm_ref[l]).astype(BF16) for l in heads]
    state = [state_ref[l] for l in heads]
    o = [_dot(scores[l], v[l]) + xi_ref[l] * _dot(qb[l], state[l].astype(BF16)) for l in heads]
    for l in heads:
        k_dec = (k[l] * zeta_ref[l]).astype(BF16)
        state_ref[l] = state[l] * gt_ref[l] + _dot(k_dec, v[l], _TN)
    for l in heads:
        mu = jnp.mean(o[l], axis=-1, keepdims=True)
        oc = o[l] - mu
        var = jnp.mean(oc * oc, axis=-1, keepdims=True)
        y = oc * lax.rsqrt(var + EPS) * gain_ref[:, sls[l]]
        o_ref[:, sls[l]] = (y * _silu(g_ref[:, sls[l]].astype(F32))).astype(o_ref.dtype)


def _retention(proj3, ret_gain, ret_width, hb=8):
    bsz, t, _ = proj3.shape
    hd = ret_width // RET_HEADS
    nhb = RET_HEADS // hb
    cw = hb * hd
    cos, sin, dmask, xi, zeta, gtile = _retention_tables(t, hd)
    col = lambda sec: pl.BlockSpec((None, TILE, cw), lambda b, h, i: (b, i, sec * nhb + h))
    head = lambda r, c: pl.BlockSpec((hb, r, c), lambda b, h, i: (h, 0, 0))
    return pl.pallas_call(
        functools.partial(_ret_body, hd=hd, hb=hb),
        grid=(bsz, nhb, t // TILE),
        in_specs=[col(0), col(1), col(2), col(3),
                  pl.BlockSpec((TILE, hd // 2), lambda b, h, i: (i, 0)),
                  pl.BlockSpec((TILE, hd // 2), lambda b, h, i: (i, 0)),
                  head(TILE, TILE), head(TILE, hd), head(TILE, hd), head(1, hd),
                  pl.BlockSpec((1, cw), lambda b, h, i: (0, h))],
        out_specs=pl.BlockSpec((None, TILE, cw), lambda b, h, i: (b, i, h)),
        out_shape=jax.ShapeDtypeStruct((bsz, t, ret_width), BF16),
        scratch_shapes=[pltpu.VMEM((hb, hd, hd), F32)],
        compiler_params=_params("parallel", "parallel", "arbitrary"),
        name="retention",
    )(proj3, proj3, proj3, proj3, cos, sin, dmask, xi, zeta, gtile,
      ret_gain.reshape(1, ret_width).astype(F32))


def _prep_body(x_ref, w_ref, o_ref, xbuf_ref, *, tt, n_qk_blocks, n_q_blocks):
    halo = SUBLANES

    @pl.when(pl.program_id(2) == 0)
    def _():
        xbuf_ref[0:halo, :] = jnp.zeros((halo, xbuf_ref.shape[1]), F32)

    x = x_ref[...].astype(F32)
    xbuf_ref[halo:halo + tt, :] = x
    w = w_ref[...]
    acc = w[CONV_WIDTH - 1:CONV_WIDTH, :] * x
    for j in range(CONV_WIDTH - 1):
        acc = acc + w[j:j + 1, :] * xbuf_ref[pl.ds(halo - (CONV_WIDTH - 1) + j, tt), :]
    xbuf_ref[0:halo, :] = x[tt - halo:tt, :]
    y = _silu(acc)
    c = pl.program_id(1)

    @pl.when(c < n_qk_blocks)
    def _():
        scale = jnp.where(c < n_q_blocks, GDN_HEAD_DIM ** -0.5, 1.0).astype(F32)
        for s in range(y.shape[1] // GDN_HEAD_DIM):
            sl = slice(s * GDN_HEAD_DIM, (s + 1) * GDN_HEAD_DIM)
            yh = y[:, sl]
            ss = jnp.sum(yh * yh, axis=-1, keepdims=True)
            o_ref[:, sl] = (yh * (lax.rsqrt(ss + EPS) * scale)).astype(o_ref.dtype)

    @pl.when(c >= n_qk_blocks)
    def _():
        o_ref[...] = y.astype(o_ref.dtype)


def _gdn_prep(proj3, conv_w, col_off, width, tt=4096, cb=512):
    bsz, t, _ = proj3.shape
    tt = min(tt, t)
    ncb = 3 * width // cb
    return pl.pallas_call(
        functools.partial(_prep_body, tt=tt, n_qk_blocks=2 * width // cb, n_q_blocks=width // cb),
        grid=(bsz, ncb, t // tt),
        in_specs=[pl.BlockSpec((None, tt, cb), lambda b, c, i: (b, i, col_off // cb + c)),
                  pl.BlockSpec((CONV_WIDTH, cb), lambda b, c, i: (0, c))],
        out_specs=pl.BlockSpec((None, tt, cb), lambda b, c, i: (b, i, c)),
        out_shape=jax.ShapeDtypeStruct((bsz, t, 3 * width), BF16),
        scratch_shapes=[pltpu.VMEM((tt + 8, cb), F32)],
        compiler_params=_params("parallel", "parallel", "arbitrary"),
        name="gdn_prep",
    )(proj3, conv_w.astype(F32))


def _gdn_body(q_ref, k_ref, v_ref, z_ref, gate_ref, incl_ref, nstrict_ref, eye_ref, blk_ref, gain_ref,
              o_ref, state_ref, *, hb, nt):
    d = GDN_HEAD_DIM
    nck = TILE // CHUNK

    @pl.when(pl.program_id(2) == 0)
    def _():
        state_ref[...] = jnp.zeros_like(state_ref)

    incl = incl_ref[...] > 0.0
    nstrict = nstrict_ref[...]
    eye = eye_ref[...]
    blk = blk_ref[...]
    gain = gain_ref[...]

    def block_diag(packed):
        return jnp.concatenate([packed] * nck, axis=0) * blk

    cols = []
    for g in range(nt):
        rows = jnp.concatenate([gate_ref[c, :, g * TILE:(g + 1) * TILE] for c in range(6)]
                               + [jnp.zeros((LANES - 6 * hb, TILE), F32)], axis=0)
        cols.append(rows.T)

    units = [(l, g) for g in range(nt) for l in range(hb)]
    un = range(len(units))
    rsl = [slice(g * TILE, (g + 1) * TILE) for (l, g) in units]
    csl = [slice(l * d, (l + 1) * d) for (l, g) in units]
    col = lambda c, u: cols[units[u][1]][:, c * hb + units[u][0]:c * hb + units[u][0] + 1]
    grow = lambda c, u: gate_ref[c, units[u][0]:units[u][0] + 1, rsl[u]]
    kbs = [k_ref[rsl[u], csl[u]] for u in un]
    ks = [kb.astype(F32) for kb in kbs]
    decay = [jnp.exp2(jnp.where(incl, col(5, u) - grow(5, u), NEG_BIG)) for u in un]
    k_beta = [ks[u] * col(0, u) for u in un]
    kkd = [_dot(k_beta[u].astype(BF16), kbs[u], _NT) * decay[u] for u in un]
    p = [sum(kkd[u][c * CHUNK:(c + 1) * CHUNK] for c in range(nck)) * nstrict for u in un]
    acc = [eye + p[u] for u in un]
    pw = [p[u].astype(BF16) for u in un]
    pw = [_dot(pw[u], block_diag(pw[u])).astype(BF16) for u in un]
    for _ in range(4):
        res = [_dot(jnp.concatenate([acc[u].astype(BF16), pw[u]], axis=0), block_diag(pw[u])) for u in un]
        acc = [acc[u] + res[u][:CHUNK] for u in un]
        pw = [res[u][CHUNK:].astype(BF16) for u in un]
    acc = [acc[u] + _dot(acc[u].astype(BF16), block_diag(pw[u])) for u in un]
    rhs = [jnp.concatenate([v_ref[rsl[u], csl[u]].astype(F32) * col(0, u), k_beta[u] * col(2, u)],
                           axis=-1).astype(BF16) for u in un]
    sol = [_dot(block_diag(acc[u].astype(BF16)), rhs[u]).astype(BF16) for u in un]
    qs = [q_ref[rsl[u], csl[u]].astype(F32) for u in un]
    attn = [(_dot(qs[u].astype(BF16), kbs[u], _NT) * decay[u]).astype(BF16) for u in un]
    au_aw = [_dot(attn[u], sol[u]) for u in un]
    q_eff = [(qs[u] * col(2, u) - au_aw[u][:, d:]).astype(BF16) for u in un]
    k_dec = [(ks[u] * col(3, u)).astype(BF16) for u in un]
    chunk = lambda c: slice(c * CHUNK, (c + 1) * CHUNK)
    ktuw = [[_dot(k_dec[u][chunk(c)], sol[u][chunk(c)], _TN) for c in range(nck)] for u in un]

    state = [state_ref[l] for l in range(hb)]
    outs = [[] for _ in un]
    for g in range(nt):
        for c in range(nck):
            r = chunk(c)
            us = [g * hb + l for l in range(hb)]
            res = [_dot(jnp.concatenate([q_eff[u][r], ktuw[u][c][:, d:].astype(BF16)], axis=0),
                        state[units[u][0]].astype(BF16)) for u in us]
            for u, rs in zip(us, res):
                l = units[u][0]
                outs[u].append(rs[:CHUNK] + au_aw[u][r, :d])
                cd = cols[g][c * CHUNK:c * CHUNK + 1, 4 * hb + l:4 * hb + l + 1]
                state[l] = state[l] * cd + ktuw[u][c][:, :d] - rs[CHUNK:]
    for l in range(hb):
        state_ref[l] = state[l]
    for u in un:
        o = jnp.concatenate(outs[u], axis=0)
        o = o * lax.rsqrt(jnp.mean(o * o, axis=-1, keepdims=True) + EPS) * gain
        o_ref[rsl[u], csl[u]] = (o * _silu(z_ref[rsl[u], csl[u]].astype(F32))).astype(o_ref.dtype)


def _gated_delta(qkv3, proj3, z_col_off, gates, norm_gain, width, hb=8, nt=4):
    bsz, t, _ = qkv3.shape
    d = GDN_HEAD_DIM
    nh = width // d
    hb = min(hb, nh)
    nhb = nh // hb
    cw = hb * d
    nt = min(nt, t // TILE)
    rows = nt * TILE
    assert t % rows == 0
    i = np.arange(TILE)[:, None]
    j = np.arange(TILE)[None, :]
    same = (i // CHUNK) == (j // CHUNK)
    incl = jnp.asarray(same & (i >= j), F32)
    blk = jnp.asarray(same, BF16)
    ip = np.arange(CHUNK)[:, None]
    nstrict = jnp.asarray(-1.0 * (ip > (j % CHUNK)), F32)
    eye = jnp.asarray(ip == (j % CHUNK), F32)
    col = lambda a, off: pl.BlockSpec((None, rows, cw), lambda b, h, s: (b, s, off // cw + h))
    const = pl.BlockSpec((TILE, TILE), lambda b, h, s: (0, 0))
    packed = pl.BlockSpec((CHUNK, TILE), lambda b, h, s: (0, 0))
    return pl.pallas_call(
        functools.partial(_gdn_body, hb=hb, nt=nt),
        grid=(bsz, nhb, t // rows),
        in_specs=[col(qkv3, 0), col(qkv3, width), col(qkv3, 2 * width), col(proj3, z_col_off),
                  pl.BlockSpec((None, 8, hb, rows), lambda b, h, s: (b, 0, h, s)),
                  const, packed, packed, const,
                  pl.BlockSpec((1, d), lambda b, h, s: (0, 0))],
        out_specs=pl.BlockSpec((None, rows, cw), lambda b, h, s: (b, s, h)),
        out_shape=jax.ShapeDtypeStruct((bsz, t, width), BF16),
        scratch_shapes=[pltpu.VMEM((hb, d, d), F32)],
        compiler_params=_params("parallel", "parallel", "arbitrary"),
        name="gated_delta",
    )(qkv3, qkv3, qkv3, proj3, gates, incl, nstrict, eye, blk, norm_gain.reshape(1, d).astype(F32))


def kernel(x, ln1_gain, w_in, ret_norm_gain, gdn_conv_w, gdn_A_log, gdn_dt_bias, gdn_norm_gain,
           w_out, ln2_gain, w_up, w_down, final_gain):
    bsz, t, dm = x.shape
    depth = w_in.shape[0]
    nh = gdn_A_log.shape[1]
    gw = nh * GDN_HEAD_DIM
    rw = w_out.shape[1] - gw
    main_cols = 4 * rw + 4 * gw
    m = bsz * t
    xf = x.reshape(m, dm)
    for l in range(depth):
        wt = w_in[l].T
        h, gates = _norm_gates(xf, ln1_gain[l], wt, main_cols, gdn_A_log[l], gdn_dt_bias[l], t)
        proj, w_i = _in_proj_first(h, wt, main_cols, BF16)
        proj, w_u, w_o = _matmul(h, w_i, BF16, casts=(w_up[l], w_out[l]), into=proj, name="in_proj")
        proj3 = proj.reshape(bsz, t, main_cols)
        y_ret = _retention(proj3, ret_norm_gain[l], rw)
        qkv = _gdn_prep(proj3, gdn_conv_w[l], 4 * rw, gw)
        y_gdn = _gated_delta(qkv, proj3, 4 * rw + 3 * gw, gates, gdn_norm_gain[l], gw)
        xf, xg, ss = _out_proj(y_ret.reshape(m, rw), y_gdn.reshape(m, gw), w_o, xf, ln2_gain[l])
        act, w_d = _matmul(xg, w_u, BF16, relu2=True, ss=ss, casts=(w_down[l],), name="up_proj")
        if l < depth - 1:
            xf, = _down_proj(act, w_d, xf, (0, m))
    split = (m // DOWN_TM) * 7 // 8 * DOWN_TM
    x_a, = _down_proj(act, w_d, xf, (0, split), tm=DOWN_TM)
    x_b, out = _down_proj(act, w_d, xf, (split, m), side=(x_a, final_gain, m), tm=DOWN_TM, tk=2048)
    out = _rmsnorm(x_b, final_gain, F32, into=(out, split))
    return out.reshape(bsz, t, dm)
```

```python
import functools

import numpy as np
import jax
import jax.numpy as jnp
from jax import lax
from jax.experimental import pallas as pl
from jax.experimental.pallas import tpu as pltpu

F32 = jnp.float32
BF16 = jnp.bfloat16

CHUNK = 64
RET_HEADS = 8
GDN_HEAD_DIM = 128
CONV_WIDTH = 4
ROPE_BASE = 10000.0
EPS = 1e-6

V7X_VMEM_LIMIT_BYTES = 56 * 1024 * 1024
LANES = 128
SUBLANES = 8
BF16_ROWS = 16
TILE = 256
DOWN_TM = 1024
NEG_BIG = -1e30
LOG2E = 1.4426950408889634

_NT = (((1,), (1,)), ((), ()))
_TN = (((0,), (0,)), ((), ()))


def _params(*sem):
    return pltpu.CompilerParams(dimension_semantics=sem, vmem_limit_bytes=V7X_VMEM_LIMIT_BYTES)


def _dot(a, b, dims=None):
    if dims is None:
        return jnp.dot(a, b, preferred_element_type=F32)
    return lax.dot_general(a, b, dims, preferred_element_type=F32)


def _silu(x):
    return x * jax.nn.sigmoid(x)


def _rms(x, gain):
    return x * lax.rsqrt(jnp.mean(x * x, axis=-1, keepdims=True) + EPS) * gain


def _rmsnorm_body(x_ref, g_ref, *rest):
    o_ref = rest[-1]
    o_ref[...] = _rms(x_ref[...].astype(F32), g_ref[...]).astype(o_ref.dtype)


def _rmsnorm(x, gain, out_dtype, tm=512, into=None):
    m, d = x.shape
    tm = min(tm, m)
    in_specs = [pl.BlockSpec((tm, d), lambda i: (i, 0)),
                pl.BlockSpec((1, d), lambda i: (0, 0))]
    args = [x, gain.reshape(1, d).astype(F32)]
    off, out_shape, aliases = 0, (m, d), {}
    if into is not None:
        dst, row0 = into
        assert row0 % tm == 0 and dst.dtype == out_dtype and dst.shape[1] == d
        off, out_shape, aliases = row0 // tm, dst.shape, {2: 0}
        in_specs.append(pl.BlockSpec(memory_space=pl.ANY))
        args.append(dst)
    return pl.pallas_call(
        _rmsnorm_body,
        grid=(m // tm,),
        in_specs=in_specs,
        out_specs=pl.BlockSpec((tm, d), lambda i: (i + off, 0)),
        out_shape=jax.ShapeDtypeStruct(out_shape, out_dtype),
        input_output_aliases=aliases,
        compiler_params=_params("parallel"),
        name="rmsnorm",
    )(*args)


def _cast_specs(casts, n_steps, nj):
    specs = []
    for w in casts:
        rows, cols = w.shape
        assert rows % (BF16_ROWS * n_steps) == 0
        specs.append(pl.BlockSpec((rows // n_steps, cols),
                                  lambda i, j: (jnp.minimum(i * nj + j, n_steps - 1), 0)))
    return specs


def _in_first_body(a_ref, bt_ref, o_ref, wb_ref):
    wb = bt_ref[...].T.astype(wb_ref.dtype)
    wb_ref[...] = wb
    o_ref[...] = _dot(a_ref[...], wb).astype(o_ref.dtype)


def _in_proj_first(a, wt, n, out_dtype, tm=1024, tn=512):
    m, k = a.shape
    tm, tn = min(tm, m), min(tn, n)
    assert m % tm == 0 and n % tn == 0
    return pl.pallas_call(
        _in_first_body,
        grid=(n // tn,),
        in_specs=[pl.BlockSpec((tm, k), lambda j: (0, 0)),
                  pl.BlockSpec((tn, k), lambda j: (j, 0))],
        out_specs=[pl.BlockSpec((tm, tn), lambda j: (0, j)),
                   pl.BlockSpec((k, tn), lambda j: (0, j))],
        out_shape=[jax.ShapeDtypeStruct((m, n), out_dtype), jax.ShapeDtypeStruct((k, n), BF16)],
        compiler_params=_params("parallel"),
        name="in_proj_first",
    )(a, wt)


def _mm_body(*refs, relu2, n_cast, has_ss, has_alias, d_norm):
    a_ref, b_ref = refs[0], refs[1]
    pos = 2
    ss_ref = None
    if has_ss:
        ss_ref, pos = refs[pos], pos + 1
    if has_alias:
        pos += 1
    cast_in = refs[pos:pos + n_cast]
    o_ref = refs[pos + n_cast]
    cast_out = refs[pos + n_cast + 1:]
    acc = _dot(a_ref[...], b_ref[...])
    if has_ss:
        acc = acc * lax.rsqrt(ss_ref[:, 0:1] * (1.0 / d_norm) + EPS)
    if relu2:
        acc = jnp.square(jnp.maximum(acc, 0.0))
    o_ref[...] = acc.astype(o_ref.dtype)
    for src, dst in zip(cast_in, cast_out):
        dst[...] = src[...].astype(dst.dtype)


def _matmul(a, b, out_dtype, *, n=None, relu2=False, ss=None, casts=(), into=None,
            tm=1024, tn=1024, name="matmul"):
    m, k = a.shape
    n = b.shape[1] if n is None else n
    tm, tn = min(tm, m), min(tn, n)
    assert m % tm == 0 and n % tn == 0
    i0 = 0 if into is None else 1
    ni, nj = m // tm - i0, n // tn
    cast_steps = 1 << ((ni * nj).bit_length() - 1)
    in_specs = [pl.BlockSpec((tm, k), lambda i, j: (i + i0, 0)),
                pl.BlockSpec((k, tn), lambda i, j: (0, j))]
    args = [a, b]
    if ss is not None:
        in_specs.append(pl.BlockSpec((tm, ss.shape[1]), lambda i, j: (i + i0, 0)))
        args.append(ss)
    aliases = {}
    if into is not None:
        assert into.shape == (m, n) and into.dtype == out_dtype
        aliases = {len(args): 0}
        in_specs.append(pl.BlockSpec(memory_space=pl.ANY))
        args.append(into)
    return pl.pallas_call(
        functools.partial(_mm_body, relu2=relu2, n_cast=len(casts), has_ss=ss is not None,
                          has_alias=into is not None, d_norm=k),
        grid=(ni, nj),
        in_specs=in_specs + _cast_specs(casts, cast_steps, nj),
        out_specs=[pl.BlockSpec((tm, tn), lambda i, j: (i + i0, j))] + _cast_specs(casts, cast_steps, nj),
        out_shape=[jax.ShapeDtypeStruct((m, n), out_dtype)]
        + [jax.ShapeDtypeStruct(w.shape, BF16) for w in casts],
        input_output_aliases=aliases,
        compiler_params=_params("parallel", "arbitrary"),
        name=name,
    )(*args, *casts)


def _out_proj_body(a1_ref, a2_ref, b1_ref, b2_ref, x_ref, g_ref, o_ref, xg_ref, ss_ref):
    x1 = x_ref[...] + (_dot(a1_ref[...], b1_ref[...]) + _dot(a2_ref[...], b2_ref[...]))
    o_ref[...] = x1
    xg_ref[...] = (x1 * g_ref[...]).astype(xg_ref.dtype)
    part = jnp.broadcast_to(jnp.sum(x1 * x1, axis=-1, keepdims=True), ss_ref.shape)

    @pl.when(pl.program_id(1) == 0)
    def _():
        ss_ref[...] = part

    @pl.when(pl.program_id(1) > 0)
    def _():
        ss_ref[...] += part


def _out_proj(y1, y2, w, x, gain, tm=1024, tn=512):
    m, k1 = y1.shape
    assert y2.shape[1] == k1 and w.shape[0] == 2 * k1
    n = w.shape[1]
    tm, tn = min(tm, m), min(tn, n)
    nj = n // tn
    tile = lambda: pl.BlockSpec((tm, tn), lambda i, j: (i, j))
    return pl.pallas_call(
        _out_proj_body,
        grid=(m // tm, nj),
        in_specs=[pl.BlockSpec((tm, k1), lambda i, j: (i, 0)),
                  pl.BlockSpec((tm, k1), lambda i, j: (i, 0)),
                  pl.BlockSpec((k1, tn), lambda i, j: (0, j)),
                  pl.BlockSpec((k1, tn), lambda i, j: (1, j)),
                  tile(),
                  pl.BlockSpec((1, tn), lambda i, j: (0, j))],
        out_specs=[tile(), tile(), pl.BlockSpec((tm, LANES), lambda i, j: (i, 0))],
        out_shape=[jax.ShapeDtypeStruct((m, n), F32), jax.ShapeDtypeStruct((m, n), BF16),
                   jax.ShapeDtypeStruct((m, LANES), F32)],
        compiler_params=_params("parallel", "arbitrary"),
        name="out_proj",
    )(y1, y2, w, w, x, gain.reshape(1, n).astype(F32))


def _down_body(a_ref, b_ref, x_ref, *rest):
    o_ref = rest[-1] if len(rest) == 1 else rest[2]
    @pl.when(pl.program_id(2) == 0)
    def _():
        o_ref[...] = x_ref[...]

    o_ref[...] += _dot(a_ref[...], b_ref[...])
    if len(rest) > 1:
        src_ref, g_ref, _, dst_ref = rest
        dst_ref[...] = _rms(src_ref[...], g_ref[...])


def _down_proj(a, b, x, rows, *, side=None, tm=1024, tn=1024, tk=4096):
    m, k = a.shape
    n = b.shape[1]
    tm, tn, tk = min(tm, m), min(tn, n), min(tk, k)
    assert rows[0] % tm == 0 and rows[1] % tm == 0
    r0, ni, nj, nk = rows[0] // tm, (rows[1] - rows[0]) // tm, n // tn, k // tk
    in_specs = [pl.BlockSpec((tm, tk), lambda i, j, kk: (i + r0, kk)),
                pl.BlockSpec((tk, tn), lambda i, j, kk: (kk, j)),
                pl.BlockSpec((tm, tn), lambda i, j, kk: (i + r0, j))]
    out_specs = [pl.BlockSpec((tm, tn), lambda i, j, kk: (i, j))]
    out_shape = [jax.ShapeDtypeStruct((ni * tm, n), F32)]
    args = [a, b, x]
    if side is not None:
        src, gain, m_total = side
        steps = ni * nj * nk
        rs = src.shape[0] // steps
        assert src.shape[0] % steps == 0 and rs % SUBLANES == 0 and src.shape[1] == n
        lin = lambda i, j, kk: ((i * nj + j) * nk + kk, 0)
        in_specs += [pl.BlockSpec((rs, n), lin), pl.BlockSpec((1, n), lambda i, j, kk: (0, 0))]
        out_specs.append(pl.BlockSpec((rs, n), lin))
        out_shape.append(jax.ShapeDtypeStruct((m_total, n), F32))
        args += [src, gain.reshape(1, n).astype(F32)]
    return pl.pallas_call(
        _down_body,
        grid=(ni, nj, nk),
        in_specs=in_specs,
        out_specs=out_specs,
        out_shape=out_shape,
        compiler_params=_params("parallel", "parallel", "arbitrary"),
        name="down_proj",
    )(*args)


def _split3(x):
    hi = x.astype(BF16)
    r1 = x - hi.astype(F32)
    mid = r1.astype(BF16)
    lo = (r1 - mid.astype(F32)).astype(BF16)
    return hi, mid, lo


def _norm_gates_body(x_ref, gain_ref, tail_ref, alog_ref, dtb_ref, ucum_ref, ublk_ref, h_ref, o_ref, *, nh):
    h = _rms(x_ref[...], gain_ref[...]).astype(h_ref.dtype)
    h_ref[...] = h
    logits = _dot(tail_ref[0:2 * nh, :].astype(h.dtype), h, _NT)
    beta = jax.nn.sigmoid(logits[:nh])
    g = -jnp.exp(alog_ref[...]) * jax.nn.softplus(logits[nh:] + dtb_ref[...])
    parts = _split3(g)
    ucum, ublk = ucum_ref[...], ublk_ref[...]
    gc = (_dot(parts[0], ucum) + _dot(parts[1], ucum)) + _dot(parts[2], ucum)
    gl = (_dot(parts[0], ublk) + _dot(parts[1], ublk)) + _dot(parts[2], ublk)
    o_ref[0] = beta
    o_ref[1] = gc
    o_ref[2] = jnp.exp(gc)
    o_ref[3] = jnp.exp(gl - gc)
    o_ref[4] = jnp.exp(gl)
    o_ref[5] = gc * LOG2E
    zero = jnp.zeros_like(beta)
    o_ref[6] = zero
    o_ref[7] = zero


def _norm_gates(x, gain, wt, n, a_log, dt_bias, t, tm=512):
    m, d = x.shape
    nh = a_log.shape[0]
    tm = min(tm, t)
    assert t % tm == 0 and tm % CHUNK == 0 and n % LANES == 0 and n + 2 * nh == wt.shape[0]
    nt = t // tm
    pos = np.arange(tm)
    same = (pos[:, None] // CHUNK) == (pos[None, :] // CHUNK)
    ucum = jnp.asarray(same & (pos[:, None] <= pos[None, :]), BF16)
    ublk = jnp.asarray(same, BF16)
    const = lambda shape: pl.BlockSpec(shape, lambda i: (0, 0))
    return pl.pallas_call(
        functools.partial(_norm_gates_body, nh=nh),
        grid=(m // tm,),
        in_specs=[pl.BlockSpec((tm, d), lambda i: (i, 0)),
                  const((1, d)),
                  pl.BlockSpec((LANES, d), lambda i: (n // LANES, 0)),
                  const((nh, 1)), const((nh, 1)), const((tm, tm)), const((tm, tm))],
        out_specs=[pl.BlockSpec((tm, d), lambda i: (i, 0)),
                   pl.BlockSpec((None, 8, nh, tm), lambda i: (i // nt, 0, 0, i % nt))],
        out_shape=[jax.ShapeDtypeStruct((m, d), BF16), jax.ShapeDtypeStruct((m // t, 8, nh, t), F32)],
        compiler_params=_params("parallel"),
        name="norm_gates",
    )(x, gain.reshape(1, d).astype(F32), wt, a_log.reshape(nh, 1).astype(F32),
      dt_bias.reshape(nh, 1).astype(F32), ucum, ublk)


def _retention_tables(t, hd):
    half = hd // 2
    inv = ROPE_BASE ** (-np.arange(half, dtype=np.float64) * (2.0 / hd))
    ang = np.arange(t, dtype=np.float64)[:, None] * inv[None, :]
    lg = np.log1p(-np.exp2(-5.0 - np.arange(RET_HEADS, dtype=np.float64)))[:, None, None]
    i = np.arange(TILE)[:, None]
    j = np.arange(TILE)[None, :]
    dmask = np.exp(lg * np.abs(i - j)) * ((j // CHUNK) <= (i // CHUNK))
    xi = np.broadcast_to(np.exp(lg * (i + 1.0)), (RET_HEADS, TILE, hd))
    zeta = np.broadcast_to(np.exp(lg * (TILE - 1.0 - i)), (RET_HEADS, TILE, hd))
    gtile = np.broadcast_to(np.exp(lg * TILE), (RET_HEADS, 1, hd))
    f = lambda a: jnp.asarray(np.ascontiguousarray(a), F32)
    return f(np.cos(ang)), f(np.sin(ang)), f(dmask), f(xi), f(zeta), f(gtile)


def _ret_body(q_ref, k_ref, v_ref, g_ref, cos_ref, sin_ref, dm_ref, xi_ref, zeta_ref, gt_ref,
              gain_ref, o_ref, state_ref, *, hd, hb, nt):
    @pl.when(pl.program_id(2) == 0)
    def _():
        state_ref[...] = jnp.zeros_like(state_ref)

    half = hd // 2

    def rope(x, c, s):
        x1, x2 = x[:, :half], x[:, half:]
        return jnp.concatenate([x1 * c - x2 * s, x1 * s + x2 * c], axis=-1)

    units = [(l, g) for g in range(nt) for l in range(hb)]
    un = range(len(units))
    rsl = [slice(g * TILE, (g + 1) * TILE) for (l, g) in units]
    csl = [slice(l * hd, (l + 1) * hd) for (l, g) in units]
    cos = [cos_ref[g * TILE:(g + 1) * TILE, :] for g in range(nt)]
    sin = [sin_ref[g * TILE:(g + 1) * TILE, :] for g in range(nt)]
    cos_k = [c * (hd ** -0.5) for c in cos]
    sin_k = [s * (hd ** -0.5) for s in sin]
    qb = [rope(q_ref[rsl[u], csl[u]].astype(F32), cos[units[u][1]], sin[units[u][1]]).astype(BF16) for u in un]
    k = [rope(k_ref[rsl[u], csl[u]].astype(F32), cos_k[units[u][1]], sin_k[units[u][1]]) for u in un]
    v = [v_ref[rsl[u], csl[u]].astype(BF16) for u in un]
    scores = [(_dot(qb[u], k[u].astype(BF16), _NT) * dm_ref[units[u][0]]).astype(BF16) for u in un]
    o = [_dot(scores[u], v[u]) for u in un]
    kv = [_dot((k[u] * zeta_ref[units[u][0]]).astype(BF16), v[u], _TN) for u in un]
    state = [state_ref[l] for l in range(hb)]
    for u in un:
        l = units[u][0]
        o[u] = o[u] + xi_ref[l] * _dot(qb[u], state[l].astype(BF16))
        state[l] = state[l] * gt_ref[l] + kv[u]
    for l in range(hb):
        state_ref[l] = state[l]
    for u in un:
        mu = jnp.mean(o[u], axis=-1, keepdims=True)
        oc = o[u] - mu
        var = jnp.mean(oc * oc, axis=-1, keepdims=True)
        y = oc * lax.rsqrt(var + EPS) * gain_ref[:, csl[u]]
        o_ref[rsl[u], csl[u]] = (y * _silu(g_ref[rsl[u], csl[u]].astype(F32))).astype(o_ref.dtype)


def _retention(proj3, ret_gain, ret_width, hb=8, nt=2):
    bsz, t, _ = proj3.shape
    hd = ret_width // RET_HEADS
    nhb = RET_HEADS // hb
    cw = hb * hd
    nt = min(nt, t // TILE)
    rows = nt * TILE
    assert t % rows == 0
    cos, sin, dmask, xi, zeta, gtile = _retention_tables(t, hd)
    col = lambda sec: pl.BlockSpec((None, rows, cw), lambda b, h, i: (b, i, sec * nhb + h))
    head = lambda r, c: pl.BlockSpec((hb, r, c), lambda b, h, i: (h, 0, 0))
    return pl.pallas_call(
        functools.partial(_ret_body, hd=hd, hb=hb, nt=nt),
        grid=(bsz, nhb, t // rows),
        in_specs=[col(0), col(1), col(2), col(3),
                  pl.BlockSpec((rows, hd // 2), lambda b, h, i: (i, 0)),
                  pl.BlockSpec((rows, hd // 2), lambda b, h, i: (i, 0)),
                  head(TILE, TILE), head(TILE, hd), head(TILE, hd), head(1, hd),
                  pl.BlockSpec((1, cw), lambda b, h, i: (0, h))],
        out_specs=pl.BlockSpec((None, rows, cw), lambda b, h, i: (b, i, h)),
        out_shape=jax.ShapeDtypeStruct((bsz, t, ret_width), BF16),
        scratch_shapes=[pltpu.VMEM((hb, hd, hd), F32)],
        compiler_params=_params("parallel", "parallel", "arbitrary"),
        name="retention",
    )(proj3, proj3, proj3, proj3, cos, sin, dmask, xi, zeta, gtile,
      ret_gain.reshape(1, ret_width).astype(F32))


def _prep_body(x_ref, w_ref, o_ref, xbuf_ref, *, tt, n_qk_blocks, n_q_blocks):
    halo = SUBLANES

    @pl.when(pl.program_id(2) == 0)
    def _():
        xbuf_ref[0:halo, :] = jnp.zeros((halo, xbuf_ref.shape[1]), F32)

    x = x_ref[...].astype(F32)
    xbuf_ref[halo:halo + tt, :] = x
    w = w_ref[...]
    acc = w[CONV_WIDTH - 1:CONV_WIDTH, :] * x
    for j in range(CONV_WIDTH - 1):
        acc = acc + w[j:j + 1, :] * xbuf_ref[pl.ds(halo - (CONV_WIDTH - 1) + j, tt), :]
    xbuf_ref[0:halo, :] = x[tt - halo:tt, :]
    y = _silu(acc)
    c = pl.program_id(1)

    @pl.when(c < n_qk_blocks)
    def _():
        scale = jnp.where(c < n_q_blocks, GDN_HEAD_DIM ** -0.5, 1.0).astype(F32)
        for s in range(y.shape[1] // GDN_HEAD_DIM):
            sl = slice(s * GDN_HEAD_DIM, (s + 1) * GDN_HEAD_DIM)
            yh = y[:, sl]
            ss = jnp.sum(yh * yh, axis=-1, keepdims=True)
            o_ref[:, sl] = (yh * (lax.rsqrt(ss + EPS) * scale)).astype(o_ref.dtype)

    @pl.when(c >= n_qk_blocks)
    def _():
        o_ref[...] = y.astype(o_ref.dtype)


def _gdn_prep(proj3, conv_w, col_off, width, tt=4096, cb=512):
    bsz, t, _ = proj3.shape
    tt = min(tt, t)
    ncb = 3 * width // cb
    return pl.pallas_call(
        functools.partial(_prep_body, tt=tt, n_qk_blocks=2 * width // cb, n_q_blocks=width // cb),
        grid=(bsz, ncb, t // tt),
        in_specs=[pl.BlockSpec((None, tt, cb), lambda b, c, i: (b, i, col_off // cb + c)),
                  pl.BlockSpec((CONV_WIDTH, cb), lambda b, c, i: (0, c))],
        out_specs=pl.BlockSpec((None, tt, cb), lambda b, c, i: (b, i, c)),
        out_shape=jax.ShapeDtypeStruct((bsz, t, 3 * width), BF16),
        scratch_shapes=[pltpu.VMEM((tt + 8, cb), F32)],
        compiler_params=_params("parallel", "parallel", "arbitrary"),
        name="gdn_prep",
    )(proj3, conv_w.astype(F32))


def _gdn_body(q_ref, k_ref, v_ref, z_ref, gate_ref, incl_ref, nstrict_ref, eye_ref, blk_ref, gain_ref,
              o_ref, state_ref, *, hb, nt):
    d = GDN_HEAD_DIM
    nck = TILE // CHUNK

    @pl.when(pl.program_id(2) == 0)
    def _():
        state_ref[...] = jnp.zeros_like(state_ref)

    incl = incl_ref[...] > 0.0
    nstrict = nstrict_ref[...]
    eye = eye_ref[...]
    blk = blk_ref[...]
    gain = gain_ref[...]

    def block_diag(packed):
        return jnp.concatenate([packed] * nck, axis=0) * blk

    cols = []
    for g in range(nt):
        rows = jnp.concatenate([gate_ref[c, :, g * TILE:(g + 1) * TILE] for c in range(6)]
                               + [jnp.zeros((LANES - 6 * hb, TILE), F32)], axis=0)
        cols.append(rows.T)

    units = [(l, g) for g in range(nt) for l in range(hb)]
    un = range(len(units))
    rsl = [slice(g * TILE, (g + 1) * TILE) for (l, g) in units]
    csl = [slice(l * d, (l + 1) * d) for (l, g) in units]
    col = lambda c, u: cols[units[u][1]][:, c * hb + units[u][0]:c * hb + units[u][0] + 1]
    grow = lambda c, u: gate_ref[c, units[u][0]:units[u][0] + 1, rsl[u]]
    kbs = [k_ref[rsl[u], csl[u]] for u in un]
    ks = [kb.astype(F32) for kb in kbs]
    decay = [jnp.exp2(jnp.where(incl, col(5, u) - grow(5, u), NEG_BIG)) for u in un]
    k_beta = [ks[u] * col(0, u) for u in un]
    kkd = [_dot(k_beta[u].astype(BF16), kbs[u], _NT) * decay[u] for u in un]
    p = [sum(kkd[u][c * CHUNK:(c + 1) * CHUNK] for c in range(nck)) * nstrict for u in un]
    acc = [eye + p[u] for u in un]
    pw = [p[u].astype(BF16) for u in un]
    pw = [_dot(pw[u], block_diag(pw[u])).astype(BF16) for u in un]
    for _ in range(4):
        res = [_dot(jnp.concatenate([acc[u].astype(BF16), pw[u]], axis=0), block_diag(pw[u])) for u in un]
        acc = [acc[u] + res[u][:CHUNK] for u in un]
        pw = [res[u][CHUNK:].astype(BF16) for u in un]
    acc = [acc[u] + _dot(acc[u].astype(BF16), block_diag(pw[u])) for u in un]
    rhs = [jnp.concatenate([v_ref[rsl[u], csl[u]].astype(F32) * col(0, u), k_beta[u] * col(2, u)],
                           axis=-1).astype(BF16) for u in un]
    sol = [_dot(block_diag(acc[u].astype(BF16)), rhs[u]).astype(BF16) for u in un]
    qs = [q_ref[rsl[u], csl[u]].astype(F32) for u in un]
    attn = [(_dot(qs[u].astype(BF16), kbs[u], _NT) * decay[u]).astype(BF16) for u in un]
    au_aw = [_dot(attn[u], sol[u]) for u in un]
    q_eff = [(qs[u] * col(2, u) - au_aw[u][:, d:]).astype(BF16) for u in un]
    k_dec = [(ks[u] * col(3, u)).astype(BF16) for u in un]
    chunk = lambda c: slice(c * CHUNK, (c + 1) * CHUNK)
    ktuw = [[_dot(k_dec[u][chunk(c)], sol[u][chunk(c)], _TN) for c in range(nck)] for u in un]

    state = [state_ref[l] for l in range(hb)]
    outs = [[] for _ in un]
    for g in range(nt):
        for c in range(nck):
            r = chunk(c)
            us = [g * hb + l for l in range(hb)]
            res = [_dot(jnp.concatenate([q_eff[u][r], ktuw[u][c][:, d:].astype(BF16)], axis=0),
                        state[units[u][0]].astype(BF16)) for u in us]
            for u, rs in zip(us, res):
                l = units[u][0]
                outs[u].append(rs[:CHUNK] + au_aw[u][r, :d])
                cd = cols[g][c * CHUNK:c * CHUNK + 1, 4 * hb + l:4 * hb + l + 1]
                state[l] = state[l] * cd + ktuw[u][c][:, :d] - rs[CHUNK:]
    for l in range(hb):
        state_ref[l] = state[l]
    for u in un:
        o = jnp.concatenate(outs[u], axis=0)
        o = o * lax.rsqrt(jnp.mean(o * o, axis=-1, keepdims=True) + EPS) * gain
        o_ref[rsl[u], csl[u]] = (o * _silu(z_ref[rsl[u], csl[u]].astype(F32))).astype(o_ref.dtype)


def _gated_delta(qkv3, proj3, z_col_off, gates, norm_gain, width, hb=8, nt=4):
    bsz, t, _ = qkv3.shape
    d = GDN_HEAD_DIM
    nh = width // d
    hb = min(hb, nh)
    nhb = nh // hb
    cw = hb * d
    nt = min(nt, t // TILE)
    rows = nt * TILE
    assert t % rows == 0
    i = np.arange(TILE)[:, None]
    j = np.arange(TILE)[None, :]
    same = (i // CHUNK) == (j // CHUNK)
    incl = jnp.asarray(same & (i >= j), F32)
    blk = jnp.asarray(same, BF16)
    ip = np.arange(CHUNK)[:, None]
    nstrict = jnp.asarray(-1.0 * (ip > (j % CHUNK)), F32)
    eye = jnp.asarray(ip == (j % CHUNK), F32)
    col = lambda a, off: pl.BlockSpec((None, rows, cw), lambda b, h, s: (b, s, off // cw + h))
    const = pl.BlockSpec((TILE, TILE), lambda b, h, s: (0, 0))
    packed = pl.BlockSpec((CHUNK, TILE), lambda b, h, s: (0, 0))
    return pl.pallas_call(
        functools.partial(_gdn_body, hb=hb, nt=nt),
        grid=(bsz, nhb, t // rows),
        in_specs=[col(qkv3, 0), col(qkv3, width), col(qkv3, 2 * width), col(proj3, z_col_off),
                  pl.BlockSpec((None, 8, hb, rows), lambda b, h, s: (b, 0, h, s)),
                  const, packed, packed, const,
                  pl.BlockSpec((1, d), lambda b, h, s: (0, 0))],
        out_specs=pl.BlockSpec((None, rows, cw), lambda b, h, s: (b, s, h)),
        out_shape=jax.ShapeDtypeStruct((bsz, t, width), BF16),
        scratch_shapes=[pltpu.VMEM((hb, d, d), F32)],
        compiler_params=_params("parallel", "parallel", "arbitrary"),
        name="gated_delta",
    )(qkv3, qkv3, qkv3, proj3, gates, incl, nstrict, eye, blk, norm_gain.reshape(1, d).astype(F32))


def kernel(x, ln1_gain, w_in, ret_norm_gain, gdn_conv_w, gdn_A_log, gdn_dt_bias, gdn_norm_gain,
           w_out, ln2_gain, w_up, w_down, final_gain):
    bsz, t, dm = x.shape
    depth = w_in.shape[0]
    nh = gdn_A_log.shape[1]
    gw = nh * GDN_HEAD_DIM
    rw = w_out.shape[1] - gw
    main_cols = 4 * rw + 4 * gw
    m = bsz * t
    xf = x.reshape(m, dm)
    for l in range(depth):
        wt = w_in[l].T
        h, gates = _norm_gates(xf, ln1_gain[l], wt, main_cols, gdn_A_log[l], gdn_dt_bias[l], t)
        proj, w_i = _in_proj_first(h, wt, main_cols, BF16)
        proj, w_u, w_o = _matmul(h, w_i, BF16, casts=(w_up[l], w_out[l]), into=proj, name="in_proj")
        proj3 = proj.reshape(bsz, t, main_cols)
        y_ret = _retention(proj3, ret_norm_gain[l], rw)
        qkv = _gdn_prep(proj3, gdn_conv_w[l], 4 * rw, gw)
        y_gdn = _gated_delta(qkv, proj3, 4 * rw + 3 * gw, gates, gdn_norm_gain[l], gw)
        xf, xg, ss = _out_proj(y_ret.reshape(m, rw), y_gdn.reshape(m, gw), w_o, xf, ln2_gain[l])
        act, w_d = _matmul(xg, w_u, BF16, relu2=True, ss=ss, casts=(w_down[l],), name="up_proj")
        if l < depth - 1:
            xf, = _down_proj(act, w_d, xf, (0, m))
    split = (m // DOWN_TM) * 7 // 8 * DOWN_TM
    x_a, = _down_proj(act, w_d, xf, (0, split), tm=DOWN_TM)
    x_b, out = _down_proj(act, w_d, xf, (split, m), side=(x_a, final_gain, m), tm=DOWN_TM, tk=2048)
    out = _rmsnorm(x_b, final_gain, F32, into=(out, split))
    return out.reshape(bsz, t, dm)
```

```python
import functools

import numpy as np
import jax
import jax.numpy as jnp
from jax import lax
from jax.experimental import pallas as pl
from jax.experimental.pallas import tpu as pltpu

F32 = jnp.float32
BF16 = jnp.bfloat16

CHUNK = 64
RET_HEADS = 8
GDN_HEAD_DIM = 128
CONV_WIDTH = 4
ROPE_BASE = 10000.0
EPS = 1e-6

V7X_VMEM_LIMIT_BYTES = 56 * 1024 * 1024
LANES = 128
SUBLANES = 8
BF16_ROWS = 16
TILE = 256
DOWN_TM = 1024
NEG_BIG = -1e30
LOG2E = 1.4426950408889634

_NT = (((1,), (1,)), ((), ()))
_TN = (((0,), (0,)), ((), ()))


def _params(*sem):
    return pltpu.CompilerParams(dimension_semantics=sem, vmem_limit_bytes=V7X_VMEM_LIMIT_BYTES)


def _dot(a, b, dims=None):
    if dims is None:
        return jnp.dot(a, b, preferred_element_type=F32)
    return lax.dot_general(a, b, dims, preferred_element_type=F32)


def _silu(x):
    return x * jax.nn.sigmoid(x)


def _rms(x, gain):
    return x * lax.rsqrt(jnp.mean(x * x, axis=-1, keepdims=True) + EPS) * gain


def _rmsnorm_body(x_ref, g_ref, *rest):
    o_ref = rest[-1]
    o_ref[...] = _rms(x_ref[...].astype(F32), g_ref[...]).astype(o_ref.dtype)


def _rmsnorm(x, gain, out_dtype, tm=512, into=None):
    m, d = x.shape
    tm = min(tm, m)
    in_specs = [pl.BlockSpec((tm, d), lambda i: (i, 0)),
                pl.BlockSpec((1, d), lambda i: (0, 0))]
    args = [x, gain.reshape(1, d).astype(F32)]
    off, out_shape, aliases = 0, (m, d), {}
    if into is not None:
        dst, row0 = into
        assert row0 % tm == 0 and dst.dtype == out_dtype and dst.shape[1] == d
        off, out_shape, aliases = row0 // tm, dst.shape, {2: 0}
        in_specs.append(pl.BlockSpec(memory_space=pl.ANY))
        args.append(dst)
    return pl.pallas_call(
        _rmsnorm_body,
        grid=(m // tm,),
        in_specs=in_specs,
        out_specs=pl.BlockSpec((tm, d), lambda i: (i + off, 0)),
        out_shape=jax.ShapeDtypeStruct(out_shape, out_dtype),
        input_output_aliases=aliases,
        compiler_params=_params("parallel"),
        name="rmsnorm",
    )(*args)


def _cast_specs(casts, n_steps, nj):
    specs = []
    for w in casts:
        rows, cols = w.shape
        assert rows % (BF16_ROWS * n_steps) == 0
        specs.append(pl.BlockSpec((rows // n_steps, cols),
                                  lambda i, j: (jnp.minimum(i * nj + j, n_steps - 1), 0)))
    return specs


def _in_first_body(a_ref, bt_ref, o_ref, wb_ref):
    wb = bt_ref[...].T.astype(wb_ref.dtype)
    wb_ref[...] = wb
    o_ref[...] = _dot(a_ref[...], wb).astype(o_ref.dtype)


def _in_proj_first(a, wt, n, out_dtype, tm=1024, tn=512):
    m, k = a.shape
    tm, tn = min(tm, m), min(tn, n)
    assert m % tm == 0 and n % tn == 0
    return pl.pallas_call(
        _in_first_body,
        grid=(n // tn,),
        in_specs=[pl.BlockSpec((tm, k), lambda j: (0, 0)),
                  pl.BlockSpec((tn, k), lambda j: (j, 0))],
        out_specs=[pl.BlockSpec((tm, tn), lambda j: (0, j)),
                   pl.BlockSpec((k, tn), lambda j: (0, j))],
        out_shape=[jax.ShapeDtypeStruct((m, n), out_dtype), jax.ShapeDtypeStruct((k, n), BF16)],
        compiler_params=_params("parallel"),
        name="in_proj_first",
    )(a, wt)


def _mm_body(*refs, relu2, n_cast, has_ss, has_alias, d_norm):
    a_ref, b_ref = refs[0], refs[1]
    pos = 2
    ss_ref = None
    if has_ss:
        ss_ref, pos = refs[pos], pos + 1
    if has_alias:
        pos += 1
    cast_in = refs[pos:pos + n_cast]
    o_ref = refs[pos + n_cast]
    cast_out = refs[pos + n_cast + 1:]
    acc = _dot(a_ref[...], b_ref[...])
    if has_ss:
        acc = acc * lax.rsqrt(ss_ref[:, 0:1] * (1.0 / d_norm) + EPS)
    if relu2:
        acc = jnp.square(jnp.maximum(acc, 0.0))
    o_ref[...] = acc.astype(o_ref.dtype)
    for src, dst in zip(cast_in, cast_out):
        dst[...] = src[...].astype(dst.dtype)


def _matmul(a, b, out_dtype, *, n=None, relu2=False, ss=None, casts=(), into=None, emit=False,
            tm=1024, tn=1024, name="matmul"):
    m, k = a.shape
    n = b.shape[1] if n is None else n
    tm, tn = min(tm, m), min(tn, n)
    assert m % tm == 0 and n % tn == 0
    i0 = 0 if into is None else 1
    ni, nj = m // tm - i0, n // tn
    cast_steps = 1 << ((ni * nj).bit_length() - 1)
    in_specs = [pl.BlockSpec((tm, k), lambda i, j: (i + i0, 0)),
                pl.BlockSpec((k, tn), lambda i, j: (0, j))]
    args = [a, b]
    if ss is not None:
        in_specs.append(pl.BlockSpec((tm, ss.shape[1]), lambda i, j: (i + i0, 0)))
        args.append(ss)
    aliases = {}
    if into is not None:
        assert into.shape == (m, n) and into.dtype == out_dtype
        aliases = {len(args): 0}
        in_specs.append(pl.BlockSpec(memory_space=pl.ANY))
        args.append(into)
    body = functools.partial(_mm_body, relu2=relu2, n_cast=len(casts), has_ss=ss is not None,
                             has_alias=into is not None, d_norm=k)
    in_specs = in_specs + _cast_specs(casts, cast_steps, nj)
    out_specs = [pl.BlockSpec((tm, tn), lambda i, j: (i + i0, j))] + _cast_specs(casts, cast_steps, nj)
    out_shape = [jax.ShapeDtypeStruct((m, n), out_dtype)] + [jax.ShapeDtypeStruct(w.shape, BF16) for w in casts]
    if emit:
        assert into is None

        def outer(*hbm_refs):
            pltpu.emit_pipeline(body, grid=(ni, nj), in_specs=in_specs, out_specs=out_specs)(*hbm_refs)

        any_spec = lambda: pl.BlockSpec(memory_space=pl.ANY)
        return pl.pallas_call(
            outer,
            in_specs=[any_spec() for _ in range(len(args) + len(casts))],
            out_specs=[any_spec() for _ in out_shape],
            out_shape=out_shape,
            compiler_params=pltpu.CompilerParams(vmem_limit_bytes=V7X_VMEM_LIMIT_BYTES),
            name=name,
        )(*args, *casts)
    return pl.pallas_call(
        body,
        grid=(ni, nj),
        in_specs=in_specs,
        out_specs=out_specs,
        out_shape=out_shape,
        input_output_aliases=aliases,
        compiler_params=_params("parallel", "arbitrary"),
        name=name,
    )(*args, *casts)


def _out_proj_body(a1_ref, a2_ref, b1_ref, b2_ref, x_ref, g_ref, o_ref, xg_ref, ss_ref):
    x1 = x_ref[...] + (_dot(a1_ref[...], b1_ref[...]) + _dot(a2_ref[...], b2_ref[...]))
    o_ref[...] = x1
    xg_ref[...] = (x1 * g_ref[...]).astype(xg_ref.dtype)
    part = jnp.broadcast_to(jnp.sum(x1 * x1, axis=-1, keepdims=True), ss_ref.shape)

    @pl.when(pl.program_id(1) == 0)
    def _():
        ss_ref[...] = part

    @pl.when(pl.program_id(1) > 0)
    def _():
        ss_ref[...] += part


def _out_proj(y1, y2, w, x, gain, tm=1024, tn=512):
    m, k1 = y1.shape
    assert y2.shape[1] == k1 and w.shape[0] == 2 * k1
    n = w.shape[1]
    tm, tn = min(tm, m), min(tn, n)
    nj = n // tn
    tile = lambda: pl.BlockSpec((tm, tn), lambda i, j: (i, j))
    return pl.pallas_call(
        _out_proj_body,
        grid=(m // tm, nj),
        in_specs=[pl.BlockSpec((tm, k1), lambda i, j: (i, 0)),
                  pl.BlockSpec((tm, k1), lambda i, j: (i, 0)),
                  pl.BlockSpec((k1, tn), lambda i, j: (0, j)),
                  pl.BlockSpec((k1, tn), lambda i, j: (1, j)),
                  tile(),
                  pl.BlockSpec((1, tn), lambda i, j: (0, j))],
        out_specs=[tile(), tile(), pl.BlockSpec((tm, LANES), lambda i, j: (i, 0))],
        out_shape=[jax.ShapeDtypeStruct((m, n), F32), jax.ShapeDtypeStruct((m, n), BF16),
                   jax.ShapeDtypeStruct((m, LANES), F32)],
        compiler_params=_params("parallel", "arbitrary"),
        name="out_proj",
    )(y1, y2, w, w, x, gain.reshape(1, n).astype(F32))


def _down_body(a_ref, b_ref, x_ref, *rest):
    o_ref = rest[-1] if len(rest) == 1 else rest[2]
    @pl.when(pl.program_id(2) == 0)
    def _():
        o_ref[...] = x_ref[...]

    o_ref[...] += _dot(a_ref[...], b_ref[...])
    if len(rest) > 1:
        src_ref, g_ref, _, dst_ref = rest
        dst_ref[...] = _rms(src_ref[...], g_ref[...])


def _down_proj(a, b, x, rows, *, side=None, tm=1024, tn=1024, tk=4096):
    m, k = a.shape
    n = b.shape[1]
    tm, tn, tk = min(tm, m), min(tn, n), min(tk, k)
    assert rows[0] % tm == 0 and rows[1] % tm == 0
    r0, ni, nj, nk = rows[0] // tm, (rows[1] - rows[0]) // tm, n // tn, k // tk
    in_specs = [pl.BlockSpec((tm, tk), lambda i, j, kk: (i + r0, kk)),
                pl.BlockSpec((tk, tn), lambda i, j, kk: (kk, j)),
                pl.BlockSpec((tm, tn), lambda i, j, kk: (i + r0, j))]
    out_specs = [pl.BlockSpec((tm, tn), lambda i, j, kk: (i, j))]
    out_shape = [jax.ShapeDtypeStruct((ni * tm, n), F32)]
    args = [a, b, x]
    if side is not None:
        src, gain, m_total = side
        steps = ni * nj * nk
        rs = src.shape[0] // steps
        assert src.shape[0] % steps == 0 and rs % SUBLANES == 0 and src.shape[1] == n
        lin = lambda i, j, kk: ((i * nj + j) * nk + kk, 0)
        in_specs += [pl.BlockSpec((rs, n), lin), pl.BlockSpec((1, n), lambda i, j, kk: (0, 0))]
        out_specs.append(pl.BlockSpec((rs, n), lin))
        out_shape.append(jax.ShapeDtypeStruct((m_total, n), F32))
        args += [src, gain.reshape(1, n).astype(F32)]
    return pl.pallas_call(
        _down_body,
        grid=(ni, nj, nk),
        in_specs=in_specs,
        out_specs=out_specs,
        out_shape=out_shape,
        compiler_params=_params("parallel", "parallel", "arbitrary"),
        name="down_proj",
    )(*args)


def _split3(x):
    hi = x.astype(BF16)
    r1 = x - hi.astype(F32)
    mid = r1.astype(BF16)
    lo = (r1 - mid.astype(F32)).astype(BF16)
    return hi, mid, lo


def _norm_gates_body(x_ref, gain_ref, tail_ref, alog_ref, dtb_ref, ucum_ref, ublk_ref, h_ref, o_ref, *, nh):
    h = _rms(x_ref[...], gain_ref[...]).astype(h_ref.dtype)
    h_ref[...] = h
    logits = _dot(tail_ref[0:2 * nh, :].astype(h.dtype), h, _NT)
    beta = jax.nn.sigmoid(logits[:nh])
    g = -jnp.exp(alog_ref[...]) * jax.nn.softplus(logits[nh:] + dtb_ref[...])
    parts = _split3(g)
    ucum, ublk = ucum_ref[...], ublk_ref[...]
    gc = (_dot(parts[0], ucum) + _dot(parts[1], ucum)) + _dot(parts[2], ucum)
    gl = (_dot(parts[0], ublk) + _dot(parts[1], ublk)) + _dot(parts[2], ublk)
    o_ref[0] = beta
    o_ref[1] = gc
    o_ref[2] = jnp.exp(gc)
    o_ref[3] = jnp.exp(gl - gc)
    o_ref[4] = jnp.exp(gl)
    o_ref[5] = gc * LOG2E
    zero = jnp.zeros_like(beta)
    o_ref[6] = zero
    o_ref[7] = zero


def _norm_gates(x, gain, wt, n, a_log, dt_bias, t, tm=512):
    m, d = x.shape
    nh = a_log.shape[0]
    tm = min(tm, t)
    assert t % tm == 0 and tm % CHUNK == 0 and n % LANES == 0 and n + 2 * nh == wt.shape[0]
    nt = t // tm
    pos = np.arange(tm)
    same = (pos[:, None] // CHUNK) == (pos[None, :] // CHUNK)
    ucum = jnp.asarray(same & (pos[:, None] <= pos[None, :]), BF16)
    ublk = jnp.asarray(same, BF16)
    const = lambda shape: pl.BlockSpec(shape, lambda i: (0, 0))
    return pl.pallas_call(
        functools.partial(_norm_gates_body, nh=nh),
        grid=(m // tm,),
        in_specs=[pl.BlockSpec((tm, d), lambda i: (i, 0)),
                  const((1, d)),
                  pl.BlockSpec((LANES, d), lambda i: (n // LANES, 0)),
                  const((nh, 1)), const((nh, 1)), const((tm, tm)), const((tm, tm))],
        out_specs=[pl.BlockSpec((tm, d), lambda i: (i, 0)),
                   pl.BlockSpec((None, 8, nh, tm), lambda i: (i // nt, 0, 0, i % nt))],
        out_shape=[jax.ShapeDtypeStruct((m, d), BF16), jax.ShapeDtypeStruct((m // t, 8, nh, t), F32)],
        compiler_params=_params("parallel"),
        name="norm_gates",
    )(x, gain.reshape(1, d).astype(F32), wt, a_log.reshape(nh, 1).astype(F32),
      dt_bias.reshape(nh, 1).astype(F32), ucum, ublk)


def _retention_tables(t, hd):
    half = hd // 2
    inv = ROPE_BASE ** (-np.arange(half, dtype=np.float64) * (2.0 / hd))
    ang = np.arange(t, dtype=np.float64)[:, None] * inv[None, :]
    lg = np.log1p(-np.exp2(-5.0 - np.arange(RET_HEADS, dtype=np.float64)))[:, None, None]
    i = np.arange(TILE)[:, None]
    j = np.arange(TILE)[None, :]
    dmask = np.exp(lg * np.abs(i - j)) * ((j // CHUNK) <= (i // CHUNK))
    xi = np.broadcast_to(np.exp(lg * (i + 1.0)), (RET_HEADS, TILE, hd))
    zeta = np.broadcast_to(np.exp(lg * (TILE - 1.0 - i)), (RET_HEADS, TILE, hd))
    gtile = np.broadcast_to(np.exp(lg * TILE), (RET_HEADS, 1, hd))
    f = lambda a: jnp.asarray(np.ascontiguousarray(a), F32)
    return f(np.cos(ang)), f(np.sin(ang)), f(dmask), f(xi), f(zeta), f(gtile)


def _ret_body(q_ref, k_ref, v_ref, g_ref, cos_ref, sin_ref, dm_ref, xi_ref, zeta_ref, gt_ref,
              gain_ref, o_ref, state_ref, *, hd, hb):
    @pl.when(pl.program_id(2) == 0)
    def _():
        state_ref[...] = jnp.zeros_like(state_ref)

    half = hd // 2
    cos, sin = cos_ref[...], sin_ref[...]

    def rope(x, c, s):
        x1, x2 = x[:, :half], x[:, half:]
        return jnp.concatenate([x1 * c - x2 * s, x1 * s + x2 * c], axis=-1)

    cos_k, sin_k = cos * (hd ** -0.5), sin * (hd ** -0.5)
    heads = range(hb)
    sls = [slice(l * hd, (l + 1) * hd) for l in heads]
    qb = [rope(q_ref[:, sls[l]].astype(F32), cos, sin).astype(BF16) for l in heads]
    k = [rope(k_ref[:, sls[l]].astype(F32), cos_k, sin_k) for l in heads]
    v = [v_ref[:, sls[l]].astype(BF16) for l in heads]
    scores = [(_dot(qb[l], k[l].astype(BF16), _NT) * dm_ref[l]).astype(BF16) for l in heads]
    state = [state_ref[l] for l in heads]
    o = [_dot(scores[l], v[l]) + xi_ref[l] * _dot(qb[l], state[l].astype(BF16)) for l in heads]
    for l in heads:
        k_dec = (k[l] * zeta_ref[l]).astype(BF16)
        state_ref[l] = state[l] * gt_ref[l] + _dot(k_dec, v[l], _TN)
    for l in heads:
        mu = jnp.mean(o[l], axis=-1, keepdims=True)
        oc = o[l] - mu
        var = jnp.mean(oc * oc, axis=-1, keepdims=True)
        y = oc * lax.rsqrt(var + EPS) * gain_ref[:, sls[l]]
        o_ref[:, sls[l]] = (y * _silu(g_ref[:, sls[l]].astype(F32))).astype(o_ref.dtype)


def _retention(proj3, ret_gain, ret_width, hb=8):
    bsz, t, _ = proj3.shape
    hd = ret_width // RET_HEADS
    nhb = RET_HEADS // hb
    cw = hb * hd
    cos, sin, dmask, xi, zeta, gtile = _retention_tables(t, hd)
    col = lambda sec: pl.BlockSpec((None, TILE, cw), lambda b, h, i: (b, i, sec * nhb + h))
    head = lambda r, c: pl.BlockSpec((hb, r, c), lambda b, h, i: (h, 0, 0))
    return pl.pallas_call(
        functools.partial(_ret_body, hd=hd, hb=hb),
        grid=(bsz, nhb, t // TILE),
        in_specs=[col(0), col(1), col(2), col(3),
                  pl.BlockSpec((TILE, hd // 2), lambda b, h, i: (i, 0)),
                  pl.BlockSpec((TILE, hd // 2), lambda b, h, i: (i, 0)),
                  head(TILE, TILE), head(TILE, hd), head(TILE, hd), head(1, hd),
                  pl.BlockSpec((1, cw), lambda b, h, i: (0, h))],
        out_specs=pl.BlockSpec((None, TILE, cw), lambda b, h, i: (b, i, h)),
        out_shape=jax.ShapeDtypeStruct((bsz, t, ret_width), BF16),
        scratch_shapes=[pltpu.VMEM((hb, hd, hd), F32)],
        compiler_params=_params("parallel", "parallel", "arbitrary"),
        name="retention",
    )(proj3, proj3, proj3, proj3, cos, sin, dmask, xi, zeta, gtile,
      ret_gain.reshape(1, ret_width).astype(F32))


def _prep_body(x_ref, w_ref, o_ref, xbuf_ref, *, tt, n_qk_blocks, n_q_blocks):
    halo = SUBLANES

    @pl.when(pl.program_id(2) == 0)
    def _():
        xbuf_ref[0:halo, :] = jnp.zeros((halo, xbuf_ref.shape[1]), F32)

    x = x_ref[...].astype(F32)
    xbuf_ref[halo:halo + tt, :] = x
    w = w_ref[...]
    acc = w[CONV_WIDTH - 1:CONV_WIDTH, :] * x
    for j in range(CONV_WIDTH - 1):
        acc = acc + w[j:j + 1, :] * xbuf_ref[pl.ds(halo - (CONV_WIDTH - 1) + j, tt), :]
    xbuf_ref[0:halo, :] = x[tt - halo:tt, :]
    y = _silu(acc)
    c = pl.program_id(1)

    @pl.when(c < n_qk_blocks)
    def _():
        scale = jnp.where(c < n_q_blocks, GDN_HEAD_DIM ** -0.5, 1.0).astype(F32)
        for s in range(y.shape[1] // GDN_HEAD_DIM):
            sl = slice(s * GDN_HEAD_DIM, (s + 1) * GDN_HEAD_DIM)
            yh = y[:, sl]
            ss = jnp.sum(yh * yh, axis=-1, keepdims=True)
            o_ref[:, sl] = (yh * (lax.rsqrt(ss + EPS) * scale)).astype(o_ref.dtype)

    @pl.when(c >= n_qk_blocks)
    def _():
        o_ref[...] = y.astype(o_ref.dtype)


def _gdn_prep(proj3, conv_w, col_off, width, tt=4096, cb=512):
    bsz, t, _ = proj3.shape
    tt = min(tt, t)
    ncb = 3 * width // cb
    return pl.pallas_call(
        functools.partial(_prep_body, tt=tt, n_qk_blocks=2 * width // cb, n_q_blocks=width // cb),
        grid=(bsz, ncb, t // tt),
        in_specs=[pl.BlockSpec((None, tt, cb), lambda b, c, i: (b, i, col_off // cb + c)),
                  pl.BlockSpec((CONV_WIDTH, cb), lambda b, c, i: (0, c))],
        out_specs=pl.BlockSpec((None, tt, cb), lambda b, c, i: (b, i, c)),
        out_shape=jax.ShapeDtypeStruct((bsz, t, 3 * width), BF16),
        scratch_shapes=[pltpu.VMEM((tt + 8, cb), F32)],
        compiler_params=_params("parallel", "parallel", "arbitrary"),
        name="gdn_prep",
    )(proj3, conv_w.astype(F32))


def _gdn_body(q_ref, k_ref, v_ref, z_ref, gate_ref, incl_ref, nstrict_ref, eye_ref, blk_ref, gain_ref,
              o_ref, state_ref, *, hb, nt):
    d = GDN_HEAD_DIM
    nck = TILE // CHUNK

    @pl.when(pl.program_id(2) == 0)
    def _():
        state_ref[...] = jnp.zeros_like(state_ref)

    incl = incl_ref[...] > 0.0
    nstrict = nstrict_ref[...]
    eye = eye_ref[...]
    blk = blk_ref[...]
    gain = gain_ref[...]

    def block_diag(packed):
        return jnp.concatenate([packed] * nck, axis=0) * blk

    cols = []
    for g in range(nt):
        rows = jnp.concatenate([gate_ref[c, :, g * TILE:(g + 1) * TILE] for c in range(6)]
                               + [jnp.zeros((LANES - 6 * hb, TILE), F32)], axis=0)
        cols.append(rows.T)

    units = [(l, g) for g in range(nt) for l in range(hb)]
    un = range(len(units))
    rsl = [slice(g * TILE, (g + 1) * TILE) for (l, g) in units]
    csl = [slice(l * d, (l + 1) * d) for (l, g) in units]
    col = lambda c, u: cols[units[u][1]][:, c * hb + units[u][0]:c * hb + units[u][0] + 1]
    grow = lambda c, u: gate_ref[c, units[u][0]:units[u][0] + 1, rsl[u]]
    kbs = [k_ref[rsl[u], csl[u]] for u in un]
    ks = [kb.astype(F32) for kb in kbs]
    decay = [jnp.exp2(jnp.where(incl, col(5, u) - grow(5, u), NEG_BIG)) for u in un]
    k_beta = [ks[u] * col(0, u) for u in un]
    kkd = [_dot(k_beta[u].astype(BF16), kbs[u], _NT) * decay[u] for u in un]
    p = [sum(kkd[u][c * CHUNK:(c + 1) * CHUNK] for c in range(nck)) * nstrict for u in un]
    acc = [eye + p[u] for u in un]
    pw = [p[u].astype(BF16) for u in un]
    pw = [_dot(pw[u], block_diag(pw[u])).astype(BF16) for u in un]
    for _ in range(4):
        res = [_dot(jnp.concatenate([acc[u].astype(BF16), pw[u]], axis=0), block_diag(pw[u])) for u in un]
        acc = [acc[u] + res[u][:CHUNK] for u in un]
        pw = [res[u][CHUNK:].astype(BF16) for u in un]
    acc = [acc[u] + _dot(acc[u].astype(BF16), block_diag(pw[u])) for u in un]
    rhs = [jnp.concatenate([v_ref[rsl[u], csl[u]].astype(F32) * col(0, u), k_beta[u] * col(2, u)],
                           axis=-1).astype(BF16) for u in un]
    sol = [_dot(block_diag(acc[u].astype(BF16)), rhs[u]).astype(BF16) for u in un]
    qs = [q_ref[rsl[u], csl[u]].astype(F32) for u in un]
    attn = [(_dot(qs[u].astype(BF16), kbs[u], _NT) * decay[u]).astype(BF16) for u in un]
    au_aw = [_dot(attn[u], sol[u]) for u in un]
    q_eff = [(qs[u] * col(2, u) - au_aw[u][:, d:]).astype(BF16) for u in un]
    k_dec = [(ks[u] * col(3, u)).astype(BF16) for u in un]
    chunk = lambda c: slice(c * CHUNK, (c + 1) * CHUNK)
    ktuw = [[_dot(k_dec[u][chunk(c)], sol[u][chunk(c)], _TN) for c in range(nck)] for u in un]

    state = [state_ref[l] for l in range(hb)]
    outs = [[] for _ in un]
    for g in range(nt):
        for c in range(nck):
            r = chunk(c)
            us = [g * hb + l for l in range(hb)]
            res = [_dot(jnp.concatenate([q_eff[u][r], ktuw[u][c][:, d:].astype(BF16)], axis=0),
                        state[units[u][0]].astype(BF16)) for u in us]
            for u, rs in zip(us, res):
                l = units[u][0]
                outs[u].append(rs[:CHUNK] + au_aw[u][r, :d])
                cd = cols[g][c * CHUNK:c * CHUNK + 1, 4 * hb + l:4 * hb + l + 1]
                state[l] = state[l] * cd + ktuw[u][c][:, :d] - rs[CHUNK:]
    for l in range(hb):
        state_ref[l] = state[l]
    for u in un:
        o = jnp.concatenate(outs[u], axis=0)
        o = o * lax.rsqrt(jnp.mean(o * o, axis=-1, keepdims=True) + EPS) * gain
        o_ref[rsl[u], csl[u]] = (o * _silu(z_ref[rsl[u], csl[u]].astype(F32))).astype(o_ref.dtype)


def _gated_delta(qkv3, proj3, z_col_off, gates, norm_gain, width, hb=8, nt=4):
    bsz, t, _ = qkv3.shape
    d = GDN_HEAD_DIM
    nh = width // d
    hb = min(hb, nh)
    nhb = nh // hb
    cw = hb * d
    nt = min(nt, t // TILE)
    rows = nt * TILE
    assert t % rows == 0
    i = np.arange(TILE)[:, None]
    j = np.arange(TILE)[None, :]
    same = (i // CHUNK) == (j // CHUNK)
    incl = jnp.asarray(same & (i >= j), F32)
    blk = jnp.asarray(same, BF16)
    ip = np.arange(CHUNK)[:, None]
    nstrict = jnp.asarray(-1.0 * (ip > (j % CHUNK)), F32)
    eye = jnp.asarray(ip == (j % CHUNK), F32)
    col = lambda a, off: pl.BlockSpec((None, rows, cw), lambda b, h, s: (b, s, off // cw + h))
    const = pl.BlockSpec((TILE, TILE), lambda b, h, s: (0, 0))
    packed = pl.BlockSpec((CHUNK, TILE), lambda b, h, s: (0, 0))
    return pl.pallas_call(
        functools.partial(_gdn_body, hb=hb, nt=nt),
        grid=(bsz, nhb, t // rows),
        in_specs=[col(qkv3, 0), col(qkv3, width), col(qkv3, 2 * width), col(proj3, z_col_off),
                  pl.BlockSpec((None, 8, hb, rows), lambda b, h, s: (b, 0, h, s)),
                  const, packed, packed, const,
                  pl.BlockSpec((1, d), lambda b, h, s: (0, 0))],
        out_specs=pl.BlockSpec((None, rows, cw), lambda b, h, s: (b, s, h)),
        out_shape=jax.ShapeDtypeStruct((bsz, t, width), BF16),
        scratch_shapes=[pltpu.VMEM((hb, d, d), F32)],
        compiler_params=_params("parallel", "parallel", "arbitrary"),
        name="gated_delta",
    )(qkv3, qkv3, qkv3, proj3, gates, incl, nstrict, eye, blk, norm_gain.reshape(1, d).astype(F32))


def kernel(x, ln1_gain, w_in, ret_norm_gain, gdn_conv_w, gdn_A_log, gdn_dt_bias, gdn_norm_gain,
           w_out, ln2_gain, w_up, w_down, final_gain):
    bsz, t, dm = x.shape
    depth = w_in.shape[0]
    nh = gdn_A_log.shape[1]
    gw = nh * GDN_HEAD_DIM
    rw = w_out.shape[1] - gw
    main_cols = 4 * rw + 4 * gw
    m = bsz * t
    xf = x.reshape(m, dm)
    for l in range(depth):
        wt = w_in[l].T
        h, gates = _norm_gates(xf, ln1_gain[l], wt, main_cols, gdn_A_log[l], gdn_dt_bias[l], t)
        proj, w_i = _in_proj_first(h, wt, main_cols, BF16)
        proj, w_u, w_o = _matmul(h, w_i, BF16, casts=(w_up[l], w_out[l]), into=proj, name="in_proj")
        proj3 = proj.reshape(bsz, t, main_cols)
        y_ret = _retention(proj3, ret_norm_gain[l], rw)
        qkv = _gdn_prep(proj3, gdn_conv_w[l], 4 * rw, gw)
        y_gdn = _gated_delta(qkv, proj3, 4 * rw + 3 * gw, gates, gdn_norm_gain[l], gw)
        xf, xg, ss = _out_proj(y_ret.reshape(m, rw), y_gdn.reshape(m, gw), w_o, xf, ln2_gain[l])
        act, w_d = _matmul(xg, w_u, BF16, relu2=True, ss=ss, casts=(w_down[l],), emit=True, name="up_proj")
        if l < depth - 1:
            xf, = _down_proj(act, w_d, xf, (0, m))
    split = (m // DOWN_TM) * 7 // 8 * DOWN_TM
    x_a, = _down_proj(act, w_d, xf, (0, split), tm=DOWN_TM)
    x_b, out = _down_proj(act, w_d, xf, (split, m), side=(x_a, final_gain, m), tm=DOWN_TM, tk=2048)
    out = _rmsnorm(x_b, final_gain, F32, into=(out, split))
    return out.reshape(bsz, t, dm)
```

```python
import functools

import numpy as np
import jax
import jax.numpy as jnp
from jax import lax
from jax.experimental import pallas as pl
from jax.experimental.pallas import tpu as pltpu

F32 = jnp.float32
BF16 = jnp.bfloat16

CHUNK = 64
RET_HEADS = 8
GDN_HEAD_DIM = 128
CONV_WIDTH = 4
ROPE_BASE = 10000.0
EPS = 1e-6

V7X_VMEM_LIMIT_BYTES = 56 * 1024 * 1024
LANES = 128
SUBLANES = 8
BF16_ROWS = 16
TILE = 256
DOWN_TM = 1024
FIRST_ROWS = 2048
NEG_BIG = -1e30
LOG2E = 1.4426950408889634

_NT = (((1,), (1,)), ((), ()))
_TN = (((0,), (0,)), ((), ()))


def _params(*sem):
    return pltpu.CompilerParams(dimension_semantics=sem, vmem_limit_bytes=V7X_VMEM_LIMIT_BYTES)


def _dot(a, b, dims=None):
    if dims is None:
        return jnp.dot(a, b, preferred_element_type=F32)
    return lax.dot_general(a, b, dims, preferred_element_type=F32)


def _silu(x):
    return x * jax.nn.sigmoid(x)


def _rms(x, gain):
    return x * lax.rsqrt(jnp.mean(x * x, axis=-1, keepdims=True) + EPS) * gain


def _rmsnorm_body(x_ref, g_ref, *rest):
    o_ref = rest[-1]
    o_ref[...] = _rms(x_ref[...].astype(F32), g_ref[...]).astype(o_ref.dtype)


def _rmsnorm(x, gain, out_dtype, tm=512, into=None):
    m, d = x.shape
    tm = min(tm, m)
    in_specs = [pl.BlockSpec((tm, d), lambda i: (i, 0)),
                pl.BlockSpec((1, d), lambda i: (0, 0))]
    args = [x, gain.reshape(1, d).astype(F32)]
    off, out_shape, aliases = 0, (m, d), {}
    if into is not None:
        dst, row0 = into
        assert row0 % tm == 0 and dst.dtype == out_dtype and dst.shape[1] == d
        off, out_shape, aliases = row0 // tm, dst.shape, {2: 0}
        in_specs.append(pl.BlockSpec(memory_space=pl.ANY))
        args.append(dst)
    return pl.pallas_call(
        _rmsnorm_body,
        grid=(m // tm,),
        in_specs=in_specs,
        out_specs=pl.BlockSpec((tm, d), lambda i: (i + off, 0)),
        out_shape=jax.ShapeDtypeStruct(out_shape, out_dtype),
        input_output_aliases=aliases,
        compiler_params=_params("parallel"),
        name="rmsnorm",
    )(*args)


def _cast_specs(casts, n_steps, nj):
    specs = []
    for w in casts:
        rows, cols = w.shape
        assert rows % (BF16_ROWS * n_steps) == 0
        specs.append(pl.BlockSpec((rows // n_steps, cols),
                                  lambda i, j: (jnp.minimum(i * nj + j, n_steps - 1), 0)))
    return specs


def _in_first_body(a_ref, bt_ref, o_ref, wb_ref):
    wb = bt_ref[...].T.astype(wb_ref.dtype)
    wb_ref[...] = wb
    o_ref[...] = _dot(a_ref[...], wb).astype(o_ref.dtype)


def _in_proj_first(a, wt, n, out_dtype, tm=1024, tn=512):
    m, k = a.shape
    tm, tn = min(tm, m), min(tn, n)
    assert m % tm == 0 and n % tn == 0
    return pl.pallas_call(
        _in_first_body,
        grid=(n // tn,),
        in_specs=[pl.BlockSpec((tm, k), lambda j: (0, 0), pipeline_mode=pl.Buffered(1)),
                  pl.BlockSpec((tn, k), lambda j: (j, 0))],
        out_specs=[pl.BlockSpec((tm, tn), lambda j: (0, j)),
                   pl.BlockSpec((k, tn), lambda j: (0, j))],
        out_shape=[jax.ShapeDtypeStruct((m, n), out_dtype), jax.ShapeDtypeStruct((k, n), BF16)],
        compiler_params=_params("parallel"),
        name="in_proj_first",
    )(a, wt)


def _mm_body(*refs, relu2, n_cast, has_ss, has_alias, d_norm):
    a_ref, b_ref = refs[0], refs[1]
    pos = 2
    ss_ref = None
    if has_ss:
        ss_ref, pos = refs[pos], pos + 1
    if has_alias:
        pos += 1
    cast_in = refs[pos:pos + n_cast]
    o_ref = refs[pos + n_cast]
    cast_out = refs[pos + n_cast + 1:]
    acc = _dot(a_ref[...], b_ref[...])
    if has_ss:
        acc = acc * lax.rsqrt(ss_ref[:, 0:1] * (1.0 / d_norm) + EPS)
    if relu2:
        acc = jnp.square(jnp.maximum(acc, 0.0))
    o_ref[...] = acc.astype(o_ref.dtype)
    for src, dst in zip(cast_in, cast_out):
        dst[...] = src[...].astype(dst.dtype)


def _matmul(a, b, out_dtype, *, n=None, relu2=False, ss=None, casts=(), into=None,
            tm=1024, tn=1024, name="matmul"):
    m, k = a.shape
    n = b.shape[1] if n is None else n
    tm, tn = min(tm, m), min(tn, n)
    assert m % tm == 0 and n % tn == 0
    i0 = 0
    if into is not None:
        into, filled = into
        assert filled % tm == 0
        i0 = filled // tm
    ni, nj = m // tm - i0, n // tn
    cast_steps = 1 << ((ni * nj).bit_length() - 1)
    in_specs = [pl.BlockSpec((tm, k), lambda i, j: (i + i0, 0)),
                pl.BlockSpec((k, tn), lambda i, j: (0, j))]
    args = [a, b]
    if ss is not None:
        in_specs.append(pl.BlockSpec((tm, ss.shape[1]), lambda i, j: (i + i0, 0)))
        args.append(ss)
    aliases = {}
    if into is not None:
        assert into.shape == (m, n) and into.dtype == out_dtype
        aliases = {len(args): 0}
        in_specs.append(pl.BlockSpec(memory_space=pl.ANY))
        args.append(into)
    return pl.pallas_call(
        functools.partial(_mm_body, relu2=relu2, n_cast=len(casts), has_ss=ss is not None,
                          has_alias=into is not None, d_norm=k),
        grid=(ni, nj),
        in_specs=in_specs + _cast_specs(casts, cast_steps, nj),
        out_specs=[pl.BlockSpec((tm, tn), lambda i, j: (i + i0, j))] + _cast_specs(casts, cast_steps, nj),
        out_shape=[jax.ShapeDtypeStruct((m, n), out_dtype)]
        + [jax.ShapeDtypeStruct(w.shape, BF16) for w in casts],
        input_output_aliases=aliases,
        compiler_params=_params("parallel", "arbitrary"),
        name=name,
    )(*args, *casts)


def _out_proj_body(a1_ref, a2_ref, b1_ref, b2_ref, x_ref, g_ref, o_ref, xg_ref, ss_ref):
    x1 = x_ref[...] + (_dot(a1_ref[...], b1_ref[...]) + _dot(a2_ref[...], b2_ref[...]))
    o_ref[...] = x1
    xg_ref[...] = (x1 * g_ref[...]).astype(xg_ref.dtype)
    part = jnp.broadcast_to(jnp.sum(x1 * x1, axis=-1, keepdims=True), ss_ref.shape)

    @pl.when(pl.program_id(1) == 0)
    def _():
        ss_ref[...] = part

    @pl.when(pl.program_id(1) > 0)
    def _():
        ss_ref[...] += part


def _out_proj(y1, y2, w, x, gain, tm=1024, tn=512):
    m, k1 = y1.shape
    assert y2.shape[1] == k1 and w.shape[0] == 2 * k1
    n = w.shape[1]
    tm, tn = min(tm, m), min(tn, n)
    nj = n // tn
    tile = lambda: pl.BlockSpec((tm, tn), lambda i, j: (i, j))
    return pl.pallas_call(
        _out_proj_body,
        grid=(m // tm, nj),
        in_specs=[pl.BlockSpec((tm, k1), lambda i, j: (i, 0)),
                  pl.BlockSpec((tm, k1), lambda i, j: (i, 0)),
                  pl.BlockSpec((k1, tn), lambda i, j: (0, j)),
                  pl.BlockSpec((k1, tn), lambda i, j: (1, j)),
                  tile(),
                  pl.BlockSpec((1, tn), lambda i, j: (0, j))],
        out_specs=[tile(), tile(), pl.BlockSpec((tm, LANES), lambda i, j: (i, 0))],
        out_shape=[jax.ShapeDtypeStruct((m, n), F32), jax.ShapeDtypeStruct((m, n), BF16),
                   jax.ShapeDtypeStruct((m, LANES), F32)],
        compiler_params=_params("parallel", "arbitrary"),
        name="out_proj",
    )(y1, y2, w, w, x, gain.reshape(1, n).astype(F32))


def _down_body(a_ref, b_ref, x_ref, *rest):
    o_ref = rest[-1] if len(rest) == 1 else rest[2]
    @pl.when(pl.program_id(2) == 0)
    def _():
        o_ref[...] = x_ref[...]

    o_ref[...] += _dot(a_ref[...], b_ref[...])
    if len(rest) > 1:
        src_ref, g_ref, _, dst_ref = rest
        dst_ref[...] = _rms(src_ref[...], g_ref[...])


def _down_proj(a, b, x, rows, *, side=None, tm=1024, tn=1024, tk=4096):
    m, k = a.shape
    n = b.shape[1]
    tm, tn, tk = min(tm, m), min(tn, n), min(tk, k)
    assert rows[0] % tm == 0 and rows[1] % tm == 0
    r0, ni, nj, nk = rows[0] // tm, (rows[1] - rows[0]) // tm, n // tn, k // tk
    in_specs = [pl.BlockSpec((tm, tk), lambda i, j, kk: (i + r0, kk)),
                pl.BlockSpec((tk, tn), lambda i, j, kk: (kk, j)),
                pl.BlockSpec((tm, tn), lambda i, j, kk: (i + r0, j))]
    out_specs = [pl.BlockSpec((tm, tn), lambda i, j, kk: (i, j))]
    out_shape = [jax.ShapeDtypeStruct((ni * tm, n), F32)]
    args = [a, b, x]
    if side is not None:
        src, gain, m_total = side
        steps = ni * nj * nk
        rs = src.shape[0] // steps
        assert src.shape[0] % steps == 0 and rs % SUBLANES == 0 and src.shape[1] == n
        lin = lambda i, j, kk: ((i * nj + j) * nk + kk, 0)
        in_specs += [pl.BlockSpec((rs, n), lin), pl.BlockSpec((1, n), lambda i, j, kk: (0, 0))]
        out_specs.append(pl.BlockSpec((rs, n), lin))
        out_shape.append(jax.ShapeDtypeStruct((m_total, n), F32))
        args += [src, gain.reshape(1, n).astype(F32)]
    return pl.pallas_call(
        _down_body,
        grid=(ni, nj, nk),
        in_specs=in_specs,
        out_specs=out_specs,
        out_shape=out_shape,
        compiler_params=_params("parallel", "parallel", "arbitrary"),
        name="down_proj",
    )(*args)


def _split3(x):
    hi = x.astype(BF16)
    r1 = x - hi.astype(F32)
    mid = r1.astype(BF16)
    lo = (r1 - mid.astype(F32)).astype(BF16)
    return hi, mid, lo


def _norm_gates_body(x_ref, gain_ref, tail_ref, alog_ref, dtb_ref, ucum_ref, ublk_ref, h_ref, o_ref, *, nh):
    h = _rms(x_ref[...], gain_ref[...]).astype(h_ref.dtype)
    h_ref[...] = h
    logits = _dot(tail_ref[0:2 * nh, :].astype(h.dtype), h, _NT)
    beta = jax.nn.sigmoid(logits[:nh])
    g = -jnp.exp(alog_ref[...]) * jax.nn.softplus(logits[nh:] + dtb_ref[...])
    parts = _split3(g)
    ucum, ublk = ucum_ref[...], ublk_ref[...]
    gc = (_dot(parts[0], ucum) + _dot(parts[1], ucum)) + _dot(parts[2], ucum)
    gl = (_dot(parts[0], ublk) + _dot(parts[1], ublk)) + _dot(parts[2], ublk)
    o_ref[0] = beta
    o_ref[1] = gc
    o_ref[2] = jnp.exp(gc)
    o_ref[3] = jnp.exp(gl - gc)
    o_ref[4] = jnp.exp(gl)
    o_ref[5] = gc * LOG2E
    zero = jnp.zeros_like(beta)
    o_ref[6] = zero
    o_ref[7] = zero


def _norm_gates(x, gain, wt, n, a_log, dt_bias, t, tm=512):
    m, d = x.shape
    nh = a_log.shape[0]
    tm = min(tm, t)
    assert t % tm == 0 and tm % CHUNK == 0 and n % LANES == 0 and n + 2 * nh == wt.shape[0]
    nt = t // tm
    pos = np.arange(tm)
    same = (pos[:, None] // CHUNK) == (pos[None, :] // CHUNK)
    ucum = jnp.asarray(same & (pos[:, None] <= pos[None, :]), BF16)
    ublk = jnp.asarray(same, BF16)
    const = lambda shape: pl.BlockSpec(shape, lambda i: (0, 0))
    return pl.pallas_call(
        functools.partial(_norm_gates_body, nh=nh),
        grid=(m // tm,),
        in_specs=[pl.BlockSpec((tm, d), lambda i: (i, 0)),
                  const((1, d)),
                  pl.BlockSpec((LANES, d), lambda i: (n // LANES, 0)),
                  const((nh, 1)), const((nh, 1)), const((tm, tm)), const((tm, tm))],
        out_specs=[pl.BlockSpec((tm, d), lambda i: (i, 0)),
                   pl.BlockSpec((None, 8, nh, tm), lambda i: (i // nt, 0, 0, i % nt))],
        out_shape=[jax.ShapeDtypeStruct((m, d), BF16), jax.ShapeDtypeStruct((m // t, 8, nh, t), F32)],
        compiler_params=_params("parallel"),
        name="norm_gates",
    )(x, gain.reshape(1, d).astype(F32), wt, a_log.reshape(nh, 1).astype(F32),
      dt_bias.reshape(nh, 1).astype(F32), ucum, ublk)


def _retention_tables(t, hd):
    half = hd // 2
    inv = ROPE_BASE ** (-np.arange(half, dtype=np.float64) * (2.0 / hd))
    ang = np.arange(t, dtype=np.float64)[:, None] * inv[None, :]
    lg = np.log1p(-np.exp2(-5.0 - np.arange(RET_HEADS, dtype=np.float64)))[:, None, None]
    i = np.arange(TILE)[:, None]
    j = np.arange(TILE)[None, :]
    dmask = np.exp(lg * np.abs(i - j)) * ((j // CHUNK) <= (i // CHUNK))
    xi = np.broadcast_to(np.exp(lg * (i + 1.0)), (RET_HEADS, TILE, hd))
    zeta = np.broadcast_to(np.exp(lg * (TILE - 1.0 - i)), (RET_HEADS, TILE, hd))
    gtile = np.broadcast_to(np.exp(lg * TILE), (RET_HEADS, 1, hd))
    f = lambda a: jnp.asarray(np.ascontiguousarray(a), F32)
    return f(np.cos(ang)), f(np.sin(ang)), f(dmask), f(xi), f(zeta), f(gtile)


def _ret_body(q_ref, k_ref, v_ref, g_ref, cos_ref, sin_ref, dm_ref, xi_ref, zeta_ref, gt_ref,
              gain_ref, o_ref, state_ref, *, hd, hb):
    @pl.when(pl.program_id(2) == 0)
    def _():
        state_ref[...] = jnp.zeros_like(state_ref)

    half = hd // 2
    cos, sin = cos_ref[...], sin_ref[...]

    def rope(x, c, s):
        x1, x2 = x[:, :half], x[:, half:]
        return jnp.concatenate([x1 * c - x2 * s, x1 * s + x2 * c], axis=-1)

    cos_k, sin_k = cos * (hd ** -0.5), sin * (hd ** -0.5)
    heads = range(hb)
    sls = [slice(l * hd, (l + 1) * hd) for l in heads]
    qb = [rope(q_ref[:, sls[l]].astype(F32), cos, sin).astype(BF16) for l in heads]
    k = [rope(k_ref[:, sls[l]].astype(F32), cos_k, sin_k) for l in heads]
    v = [v_ref[:, sls[l]].astype(BF16) for l in heads]
    scores = [(_dot(qb[l], k[l].astype(BF16), _NT) * dm_ref[l]).astype(BF16) for l in heads]
    state = [state_ref[l] for l in heads]
    o = [_dot(scores[l], v[l]) + xi_ref[l] * _dot(qb[l], state[l].astype(BF16)) for l in heads]
    for l in heads:
        k_dec = (k[l] * zeta_ref[l]).astype(BF16)
        state_ref[l] = state[l] * gt_ref[l] + _dot(k_dec, v[l], _TN)
    for l in heads:
        mu = jnp.mean(o[l], axis=-1, keepdims=True)
        oc = o[l] - mu
        var = jnp.mean(oc * oc, axis=-1, keepdims=True)
        y = oc * lax.rsqrt(var + EPS) * gain_ref[:, sls[l]]
        o_ref[:, sls[l]] = (y * _silu(g_ref[:, sls[l]].astype(F32))).astype(o_ref.dtype)


def _retention(proj3, ret_gain, ret_width, hb=8):
    bsz, t, _ = proj3.shape
    hd = ret_width // RET_HEADS
    nhb = RET_HEADS // hb
    cw = hb * hd
    cos, sin, dmask, xi, zeta, gtile = _retention_tables(t, hd)
    col = lambda sec: pl.BlockSpec((None, TILE, cw), lambda b, h, i: (b, i, sec * nhb + h))
    head = lambda r, c: pl.BlockSpec((hb, r, c), lambda b, h, i: (h, 0, 0))
    return pl.pallas_call(
        functools.partial(_ret_body, hd=hd, hb=hb),
        grid=(bsz, nhb, t // TILE),
        in_specs=[col(0), col(1), col(2), col(3),
                  pl.BlockSpec((TILE, hd // 2), lambda b, h, i: (i, 0)),
                  pl.BlockSpec((TILE, hd // 2), lambda b, h, i: (i, 0)),
                  head(TILE, TILE), head(TILE, hd), head(TILE, hd), head(1, hd),
                  pl.BlockSpec((1, cw), lambda b, h, i: (0, h))],
        out_specs=pl.BlockSpec((None, TILE, cw), lambda b, h, i: (b, i, h)),
        out_shape=jax.ShapeDtypeStruct((bsz, t, ret_width), BF16),
        scratch_shapes=[pltpu.VMEM((hb, hd, hd), F32)],
        compiler_params=_params("parallel", "parallel", "arbitrary"),
        name="retention",
    )(proj3, proj3, proj3, proj3, cos, sin, dmask, xi, zeta, gtile,
      ret_gain.reshape(1, ret_width).astype(F32))


def _prep_body(x_ref, w_ref, o_ref, xbuf_ref, *, tt, n_qk_blocks, n_q_blocks):
    halo = SUBLANES

    @pl.when(pl.program_id(2) == 0)
    def _():
        xbuf_ref[0:halo, :] = jnp.zeros((halo, xbuf_ref.shape[1]), F32)

    x = x_ref[...].astype(F32)
    xbuf_ref[halo:halo + tt, :] = x
    w = w_ref[...]
    acc = w[CONV_WIDTH - 1:CONV_WIDTH, :] * x
    for j in range(CONV_WIDTH - 1):
        acc = acc + w[j:j + 1, :] * xbuf_ref[pl.ds(halo - (CONV_WIDTH - 1) + j, tt), :]
    xbuf_ref[0:halo, :] = x[tt - halo:tt, :]
    y = _silu(acc)
    c = pl.program_id(1)

    @pl.when(c < n_qk_blocks)
    def _():
        scale = jnp.where(c < n_q_blocks, GDN_HEAD_DIM ** -0.5, 1.0).astype(F32)
        for s in range(y.shape[1] // GDN_HEAD_DIM):
            sl = slice(s * GDN_HEAD_DIM, (s + 1) * GDN_HEAD_DIM)
            yh = y[:, sl]
            ss = jnp.sum(yh * yh, axis=-1, keepdims=True)
            o_ref[:, sl] = (yh * (lax.rsqrt(ss + EPS) * scale)).astype(o_ref.dtype)

    @pl.when(c >= n_qk_blocks)
    def _():
        o_ref[...] = y.astype(o_ref.dtype)


def _gdn_prep(proj3, conv_w, col_off, width, tt=4096, cb=512):
    bsz, t, _ = proj3.shape
    tt = min(tt, t)
    ncb = 3 * width // cb
    return pl.pallas_call(
        functools.partial(_prep_body, tt=tt, n_qk_blocks=2 * width // cb, n_q_blocks=width // cb),
        grid=(bsz, ncb, t // tt),
        in_specs=[pl.BlockSpec((None, tt, cb), lambda b, c, i: (b, i, col_off // cb + c)),
                  pl.BlockSpec((CONV_WIDTH, cb), lambda b, c, i: (0, c))],
        out_specs=pl.BlockSpec((None, tt, cb), lambda b, c, i: (b, i, c)),
        out_shape=jax.ShapeDtypeStruct((bsz, t, 3 * width), BF16),
        scratch_shapes=[pltpu.VMEM((tt + 8, cb), F32)],
        compiler_params=_params("parallel", "parallel", "arbitrary"),
        name="gdn_prep",
    )(proj3, conv_w.astype(F32))


def _gdn_body(q_ref, k_ref, v_ref, z_ref, gate_ref, incl_ref, nstrict_ref, eye_ref, blk_ref, gain_ref,
              o_ref, state_ref, *, hb, nt):
    d = GDN_HEAD_DIM
    nck = TILE // CHUNK

    @pl.when(pl.program_id(2) == 0)
    def _():
        state_ref[...] = jnp.zeros_like(state_ref)

    incl = incl_ref[...] > 0.0
    nstrict = nstrict_ref[...]
    eye = eye_ref[...]
    blk = blk_ref[...]
    gain = gain_ref[...]

    def block_diag(packed):
        return jnp.concatenate([packed] * nck, axis=0) * blk

    cols = []
    for g in range(nt):
        rows = jnp.concatenate([gate_ref[c, :, g * TILE:(g + 1) * TILE] for c in range(6)]
                               + [jnp.zeros((LANES - 6 * hb, TILE), F32)], axis=0)
        cols.append(rows.T)

    units = [(l, g) for g in range(nt) for l in range(hb)]
    un = range(len(units))
    rsl = [slice(g * TILE, (g + 1) * TILE) for (l, g) in units]
    csl = [slice(l * d, (l + 1) * d) for (l, g) in units]
    col = lambda c, u: cols[units[u][1]][:, c * hb + units[u][0]:c * hb + units[u][0] + 1]
    grow = lambda c, u: gate_ref[c, units[u][0]:units[u][0] + 1, rsl[u]]
    kbs = [k_ref[rsl[u], csl[u]] for u in un]
    ks = [kb.astype(F32) for kb in kbs]
    decay = [jnp.exp2(jnp.where(incl, col(5, u) - grow(5, u), NEG_BIG)) for u in un]
    k_beta = [ks[u] * col(0, u) for u in un]
    kkd = [_dot(k_beta[u].astype(BF16), kbs[u], _NT) * decay[u] for u in un]
    p = [sum(kkd[u][c * CHUNK:(c + 1) * CHUNK] for c in range(nck)) * nstrict for u in un]
    acc = [eye + p[u] for u in un]
    pw = [p[u].astype(BF16) for u in un]
    pw = [_dot(pw[u], block_diag(pw[u])).astype(BF16) for u in un]
    for _ in range(4):
        res = [_dot(jnp.concatenate([acc[u].astype(BF16), pw[u]], axis=0), block_diag(pw[u])) for u in un]
        acc = [acc[u] + res[u][:CHUNK] for u in un]
        pw = [res[u][CHUNK:].astype(BF16) for u in un]
    acc = [acc[u] + _dot(acc[u].astype(BF16), block_diag(pw[u])) for u in un]
    rhs = [jnp.concatenate([v_ref[rsl[u], csl[u]].astype(F32) * col(0, u), k_beta[u] * col(2, u)],
                           axis=-1).astype(BF16) for u in un]
    sol = [_dot(block_diag(acc[u].astype(BF16)), rhs[u]).astype(BF16) for u in un]
    qs = [q_ref[rsl[u], csl[u]].astype(F32) for u in un]
    attn = [(_dot(qs[u].astype(BF16), kbs[u], _NT) * decay[u]).astype(BF16) for u in un]
    au_aw = [_dot(attn[u], sol[u]) for u in un]
    q_eff = [(qs[u] * col(2, u) - au_aw[u][:, d:]).astype(BF16) for u in un]
    k_dec = [(ks[u] * col(3, u)).astype(BF16) for u in un]
    chunk = lambda c: slice(c * CHUNK, (c + 1) * CHUNK)
    ktuw = [[_dot(k_dec[u][chunk(c)], sol[u][chunk(c)], _TN) for c in range(nck)] for u in un]

    state = [state_ref[l] for l in range(hb)]
    outs = [[] for _ in un]
    for g in range(nt):
        for c in range(nck):
            r = chunk(c)
            us = [g * hb + l for l in range(hb)]
            res = [_dot(jnp.concatenate([q_eff[u][r], ktuw[u][c][:, d:].astype(BF16)], axis=0),
                        state[units[u][0]].astype(BF16)) for u in us]
            for u, rs in zip(us, res):
                l = units[u][0]
                outs[u].append(rs[:CHUNK] + au_aw[u][r, :d])
                cd = cols[g][c * CHUNK:c * CHUNK + 1, 4 * hb + l:4 * hb + l + 1]
                state[l] = state[l] * cd + ktuw[u][c][:, :d] - rs[CHUNK:]
    for l in range(hb):
        state_ref[l] = state[l]
    for u in un:
        o = jnp.concatenate(outs[u], axis=0)
        o = o * lax.rsqrt(jnp.mean(o * o, axis=-1, keepdims=True) + EPS) * gain
        o_ref[rsl[u], csl[u]] = (o * _silu(z_ref[rsl[u], csl[u]].astype(F32))).astype(o_ref.dtype)


def _gated_delta(qkv3, proj3, z_col_off, gates, norm_gain, width, hb=8, nt=4):
    bsz, t, _ = qkv3.shape
    d = GDN_HEAD_DIM
    nh = width // d
    hb = min(hb, nh)
    nhb = nh // hb
    cw = hb * d
    nt = min(nt, t // TILE)
    rows = nt * TILE
    assert t % rows == 0
    i = np.arange(TILE)[:, None]
    j = np.arange(TILE)[None, :]
    same = (i // CHUNK) == (j // CHUNK)
    incl = jnp.asarray(same & (i >= j), F32)
    blk = jnp.asarray(same, BF16)
    ip = np.arange(CHUNK)[:, None]
    nstrict = jnp.asarray(-1.0 * (ip > (j % CHUNK)), F32)
    eye = jnp.asarray(ip == (j % CHUNK), F32)
    col = lambda a, off: pl.BlockSpec((None, rows, cw), lambda b, h, s: (b, s, off // cw + h))
    const = pl.BlockSpec((TILE, TILE), lambda b, h, s: (0, 0))
    packed = pl.BlockSpec((CHUNK, TILE), lambda b, h, s: (0, 0))
    return pl.pallas_call(
        functools.partial(_gdn_body, hb=hb, nt=nt),
        grid=(bsz, nhb, t // rows),
        in_specs=[col(qkv3, 0), col(qkv3, width), col(qkv3, 2 * width), col(proj3, z_col_off),
                  pl.BlockSpec((None, 8, hb, rows), lambda b, h, s: (b, 0, h, s)),
                  const, packed, packed, const,
                  pl.BlockSpec((1, d), lambda b, h, s: (0, 0))],
        out_specs=pl.BlockSpec((None, rows, cw), lambda b, h, s: (b, s, h)),
        out_shape=jax.ShapeDtypeStruct((bsz, t, width), BF16),
        scratch_shapes=[pltpu.VMEM((hb, d, d), F32)],
        compiler_params=_params("parallel", "parallel", "arbitrary"),
        name="gated_delta",
    )(qkv3, qkv3, qkv3, proj3, gates, incl, nstrict, eye, blk, norm_gain.reshape(1, d).astype(F32))


def kernel(x, ln1_gain, w_in, ret_norm_gain, gdn_conv_w, gdn_A_log, gdn_dt_bias, gdn_norm_gain,
           w_out, ln2_gain, w_up, w_down, final_gain):
    bsz, t, dm = x.shape
    depth = w_in.shape[0]
    nh = gdn_A_log.shape[1]
    gw = nh * GDN_HEAD_DIM
    rw = w_out.shape[1] - gw
    main_cols = 4 * rw + 4 * gw
    m = bsz * t
    xf = x.reshape(m, dm)
    for l in range(depth):
        wt = w_in[l].T
        h, gates = _norm_gates(xf, ln1_gain[l], wt, main_cols, gdn_A_log[l], gdn_dt_bias[l], t)
        first = min(FIRST_ROWS, m // 2)
        proj, w_i = _in_proj_first(h, wt, main_cols, BF16, tm=first)
        proj, w_u, w_o = _matmul(h, w_i, BF16, casts=(w_up[l], w_out[l]), into=(proj, first), name="in_proj")
        proj3 = proj.reshape(bsz, t, main_cols)
        y_ret = _retention(proj3, ret_norm_gain[l], rw)
        qkv = _gdn_prep(proj3, gdn_conv_w[l], 4 * rw, gw)
        y_gdn = _gated_delta(qkv, proj3, 4 * rw + 3 * gw, gates, gdn_norm_gain[l], gw)
        xf, xg, ss = _out_proj(y_ret.reshape(m, rw), y_gdn.reshape(m, gw), w_o, xf, ln2_gain[l])
        act, w_d = _matmul(xg, w_u, BF16, relu2=True, ss=ss, casts=(w_down[l],), name="up_proj")
        if l < depth - 1:
            xf, = _down_proj(act, w_d, xf, (0, m))
    split = (m // DOWN_TM) * 7 // 8 * DOWN_TM
    x_a, = _down_proj(act, w_d, xf, (0, split), tm=DOWN_TM)
    x_b, out = _down_proj(act, w_d, xf, (split, m), side=(x_a, final_gain, m), tm=DOWN_TM, tk=2048)
    out = _rmsnorm(x_b, final_gain, F32, into=(out, split))
    return out.reshape(bsz, t, dm)
```
